```python
import math
import jax, jax.numpy as jnp
from jax import lax
import numpy as np

D_MODEL = 4096
BATCH = 2
SEQ = 4096
DEPTH = 1
DEC_BATCH = 128
DEC_SEQ = 1
PAST_LEN = 2048
PAGE_SIZE = 128

HEAD_DIM = 128
N_HEADS_A = D_MODEL // 2 // HEAD_DIM
N_KV_A = 4
N_HEADS_B = D_MODEL // 2 // (2 * HEAD_DIM)
N_KV_B = 4
G_A = N_HEADS_A // N_KV_A
G_B = N_HEADS_B // N_KV_B
N_IDX_HEADS = 32
IDX_DIM = 128
TOPK_MAX = 256
ROPE_THETA = 500000.0
ROT_DIM = HEAD_DIM // 4
D_FF = 4 * D_MODEL
Q_BLOCK = 128
EPS = 1e-6
MIX_A = N_HEADS_A * HEAD_DIM
MIX_B = N_HEADS_B * 2 * HEAD_DIM
MIX_WIDTH = MIX_A + MIX_B
SPLIT_WIDTHS = (MIX_A, N_KV_A * HEAD_DIM, N_KV_A * HEAD_DIM,
                N_HEADS_B * 2 * HEAD_DIM, N_KV_B * 2 * HEAD_DIM, N_KV_B * 2 * HEAD_DIM,
                N_IDX_HEADS * IDX_DIM, IDX_DIM, N_IDX_HEADS)
D_IN = sum(SPLIT_WIDTHS)
SPLIT_POINTS = [int(c) for c in np.cumsum(SPLIT_WIDTHS)[:-1]]

kernel_name = "hymba_dsa_diffattn_step"


def rmsnorm(x, w):
    xf = x.astype(jnp.float32)
    y = xf * lax.rsqrt(jnp.mean(xf * xf, axis=-1, keepdims=True) + EPS)
    return (y * w.astype(jnp.float32)).astype(x.dtype)


def layernorm(x, w, b):
    xf = x.astype(jnp.float32)
    mu = jnp.mean(xf, axis=-1, keepdims=True)
    xc = xf - mu
    y = xc * lax.rsqrt(jnp.mean(xc * xc, axis=-1, keepdims=True) + EPS)
    return (y * w.astype(jnp.float32) + b.astype(jnp.float32)).astype(x.dtype)


def rope_partial(x, pos):
    half = ROT_DIM // 2
    inv = jnp.power(ROPE_THETA, -jnp.arange(half, dtype=jnp.float32) / half)
    ang = pos.astype(jnp.float32)[:, None] * inv[None, :]
    cos = jnp.cos(ang)[:, None, :].astype(x.dtype)
    sin = jnp.sin(ang)[:, None, :].astype(x.dtype)
    x1, x2 = x[..., :half], x[..., half:ROT_DIM]
    return jnp.concatenate([x1 * cos - x2 * sin, x1 * sin + x2 * cos, x[..., ROT_DIM:]], axis=-1)


def project(h, w_in, pos, kn_w, kn_b):
    B, T, _ = h.shape
    z = h @ w_in
    q_a, k_a, v_a, q_b, k_b, v_b, q_i, k_i, w_i = jnp.split(z, SPLIT_POINTS, axis=-1)
    q_a = rope_partial(q_a.reshape(B, T, N_HEADS_A, HEAD_DIM), pos)
    k_a = rope_partial(k_a.reshape(B, T, N_KV_A, HEAD_DIM), pos)
    v_a = v_a.reshape(B, T, N_KV_A, HEAD_DIM)
    q_b = rope_partial(q_b.reshape(B, T, N_HEADS_B * 2, HEAD_DIM), pos).reshape(B, T, N_HEADS_B, 2, HEAD_DIM)
    k_b = rope_partial(k_b.reshape(B, T, N_KV_B * 2, HEAD_DIM), pos).reshape(B, T, N_KV_B, 2, HEAD_DIM)
    v_b = v_b.reshape(B, T, N_KV_B, 2 * HEAD_DIM)
    q_i = rope_partial(q_i.reshape(B, T, N_IDX_HEADS, IDX_DIM), pos)
    k_i = rope_partial(layernorm(k_i, kn_w, kn_b)[:, :, None, :], pos)[:, :, 0, :]
    w_i = w_i * (N_IDX_HEADS ** -0.5)
    return q_a, k_a, v_a, q_b, k_b, v_b, q_i, k_i, w_i


def indexer_topk(q_i, w_i, k_i, q_pos, topk):
    L = k_i.shape[1]
    logits = jnp.einsum('bthd,bld->bthl', q_i, k_i).astype(jnp.float32) * (IDX_DIM ** -0.5)
    score = jnp.einsum('bthl,bth->btl', jax.nn.relu(logits), w_i.astype(jnp.float32))
    admissible = jnp.arange(L)[None, :] <= q_pos[:, None]
    score = jnp.where(admissible[None], score, -jnp.inf)
    _, idx = lax.top_k(score, topk)
    valid = idx <= q_pos[None, :, None]
    return idx, valid


def sparse_attend(q_a, k_sel, v_sel, valid):
    B, T = q_a.shape[:2]
    qg = q_a.reshape(B, T, N_KV_A, G_A, HEAD_DIM)
    s = jnp.einsum('btgrd,btkgd->btgrk', qg, k_sel).astype(jnp.float32) * (HEAD_DIM ** -0.5)
    s = jnp.where(valid[:, :, None, None, :], s, -jnp.inf)
    p = jax.nn.softmax(s, axis=-1).astype(v_sel.dtype)
    o = jnp.einsum('btgrk,btkgd->btgrd', p, v_sel)
    return o.reshape(B, T, MIX_A)


def diff_attend(q_b, k_b, v_b, q_pos, lam, gn_w, lam_init):
    B, T = q_b.shape[:2]
    L = k_b.shape[1]
    qg = q_b.reshape(B, T, N_KV_B, G_B, 2, HEAD_DIM)
    s = jnp.einsum('btgrcd,blgcd->bgrctl', qg, k_b).astype(jnp.float32) * (HEAD_DIM ** -0.5)
    causal = jnp.arange(L)[None, :] <= q_pos[:, None]
    s = jnp.where(causal, s, -jnp.inf)
    p = jax.nn.softmax(s, axis=-1)
    a = p[:, :, :, 0] - lam * p[:, :, :, 1]
    o = jnp.einsum('bgrtl,blgd->btgrd', a.astype(v_b.dtype), v_b)
    o = rmsnorm(o, gn_w) * (1.0 - lam_init)
    return o.reshape(B, T, MIX_B)


def dsa_prompt(q_a, k_a, v_a, q_i, k_i, w_i):
    B, T = q_a.shape[:2]
    topk = min(TOPK_MAX, T // 4)
    b_ix = jnp.arange(B)[:, None, None]

    def block(start):
        qpos = start + jnp.arange(Q_BLOCK)
        qi = lax.dynamic_slice_in_dim(q_i, start, Q_BLOCK, axis=1)
        wi = lax.dynamic_slice_in_dim(w_i, start, Q_BLOCK, axis=1)
        qa = lax.dynamic_slice_in_dim(q_a, start, Q_BLOCK, axis=1)
        idx, valid = indexer_topk(qi, wi, k_i, qpos, topk)
        return sparse_attend(qa, k_a[b_ix, idx], v_a[b_ix, idx], valid)

    out = lax.map(block, jnp.arange(T // Q_BLOCK) * Q_BLOCK)
    return jnp.moveaxis(out, 0, 1).reshape(B, T, MIX_A)


def diff_prompt(q_b, k_b, v_b, lam, gn_w, lam_init):
    B, T = q_b.shape[:2]

    def block(start):
        qpos = start + jnp.arange(Q_BLOCK)
        qb = lax.dynamic_slice_in_dim(q_b, start, Q_BLOCK, axis=1)
        return diff_attend(qb, k_b, v_b, qpos, lam, gn_w, lam_init)

    out = lax.map(block, jnp.arange(T // Q_BLOCK) * Q_BLOCK)
    return jnp.moveaxis(out, 0, 1).reshape(B, T, MIX_B)


def gather_pages(cache, page_table):
    g = cache[page_table]
    return g.reshape((page_table.shape[0], page_table.shape[1] * PAGE_SIZE) + cache.shape[2:])


def gather_rows_paged(cache, page_table, new_rows, idx):
    past_len = page_table.shape[1] * PAGE_SIZE
    b_ix = jnp.arange(page_table.shape[0])[:, None, None]
    p_idx = jnp.minimum(idx, past_len - 1)
    phys = page_table[b_ix, p_idx // PAGE_SIZE]
    pooled = cache[phys, p_idx % PAGE_SIZE]
    n_idx = jnp.clip(idx - past_len, 0, new_rows.shape[1] - 1)
    fresh = new_rows[b_ix, n_idx]
    is_past = (idx < past_len).reshape(idx.shape + (1,) * (pooled.ndim - 3))
    return jnp.where(is_past, pooled, fresh)


def lambda_value(lq1, lk1, lq2, lk2, lam_init):
    f = jnp.float32
    return (jnp.exp(jnp.sum(lq1.astype(f) * lk1.astype(f))) - jnp.exp(jnp.sum(lq2.astype(f) * lk2.astype(f)))
            + lam_init)


def sq_relu_mlp(x, g, w_up, w_down):
    h = rmsnorm(x, g)
    return jnp.square(jax.nn.relu(h @ w_up)) @ w_down


def setup_inputs(seed: int = 0) -> dict:
    key = jax.random.key(seed)
    ks = jax.random.split(key, 24)
    f = jnp.float32
    n_pages = PAST_LEN // PAGE_SIZE
    n_used = DEC_BATCH * n_pages
    n_pool = n_used + (n_used + 3) // 4
    nrm = lambda k, shp, s=1.0: jax.random.normal(k, shp, f) * s
    page_table = jax.random.permutation(ks[7], n_pool)[:n_used].reshape(DEC_BATCH, n_pages).astype(jnp.int32)
    return {
        "x_prompt": nrm(ks[0], (BATCH, SEQ, D_MODEL)),
        "x_sample": nrm(ks[1], (DEC_BATCH, DEC_SEQ, D_MODEL)),
        "cache_k_a": nrm(ks[2], (DEPTH, n_pool, PAGE_SIZE, N_KV_A, HEAD_DIM)),
        "cache_v_a": nrm(ks[3], (DEPTH, n_pool, PAGE_SIZE, N_KV_A, HEAD_DIM)),
        "cache_k_b": nrm(ks[4], (DEPTH, n_pool, PAGE_SIZE, N_KV_B, 2, HEAD_DIM)),
        "cache_v_b": nrm(ks[5], (DEPTH, n_pool, PAGE_SIZE, N_KV_B, 2 * HEAD_DIM)),
        "cache_k_idx": nrm(ks[6], (DEPTH, n_pool, PAGE_SIZE, IDX_DIM)),
        "page_table": page_table,
        "norm1": 1.0 + nrm(ks[8], (DEPTH, D_MODEL), 0.02),
        "w_in": nrm(ks[9], (DEPTH, D_MODEL, D_IN), D_MODEL ** -0.5),
        "k_idx_norm_w": 1.0 + nrm(ks[10], (DEPTH, IDX_DIM), 0.02),
        "k_idx_norm_b": nrm(ks[11], (DEPTH, IDX_DIM), 0.02),
        "lambda_q1": nrm(ks[12], (DEPTH, HEAD_DIM), 0.1),
        "lambda_k1": nrm(ks[13], (DEPTH, HEAD_DIM), 0.1),
        "lambda_q2": nrm(ks[14], (DEPTH, HEAD_DIM), 0.1),
        "lambda_k2": nrm(ks[15], (DEPTH, HEAD_DIM), 0.1),
        "diff_norm_w": 1.0 + nrm(ks[16], (DEPTH, 2 * HEAD_DIM), 0.02),
        "w_out": nrm(ks[17], (DEPTH, MIX_WIDTH, D_MODEL), MIX_WIDTH ** -0.5),
        "norm2": 1.0 + nrm(ks[18], (DEPTH, D_MODEL), 0.02),
        "w_up": nrm(ks[19], (DEPTH, D_MODEL, D_FF), D_MODEL ** -0.5),
        "w_down": nrm(ks[20], (DEPTH, D_FF, D_MODEL), D_FF ** -0.5),
        "norm_f": 1.0 + nrm(ks[21], (D_MODEL,), 0.02),
    }


def reference(x_prompt, x_sample, cache_k_a, cache_v_a, cache_k_b, cache_v_b, cache_k_idx, page_table,
              norm1, w_in, k_idx_norm_w, k_idx_norm_b, lambda_q1, lambda_k1, lambda_q2, lambda_k2,
              diff_norm_w, w_out, norm2, w_up, w_down, norm_f):
    seq = x_prompt.shape[1]
    dec_seq = x_sample.shape[1]
    past_len = page_table.shape[1] * PAGE_SIZE
    pos_p = jnp.arange(seq)
    pos_s = past_len + jnp.arange(dec_seq)
    topk_s = min(TOPK_MAX, (past_len + dec_seq) // 4)
    xp, xs = x_prompt, x_sample
    st = [[] for _ in range(10)]
    for l in range(DEPTH):
        lam_init = 0.8 - 0.6 * math.exp(-0.3 * l)
        lam = lambda_value(lambda_q1[l], lambda_k1[l], lambda_q2[l], lambda_k2[l], lam_init)
        h = rmsnorm(xp, norm1[l])
        qa, ka, va, qb, kb, vb, qi, ki, wi = project(h, w_in[l], pos_p, k_idx_norm_w[l], k_idx_norm_b[l])
        oa = dsa_prompt(qa, ka, va, qi, ki, wi)
        ob = diff_prompt(qb, kb, vb, lam, diff_norm_w[l], lam_init)
        xp = xp + jnp.concatenate([oa, ob], axis=-1) @ w_out[l]
        xp = xp + sq_relu_mlp(xp, norm2[l], w_up[l], w_down[l])
        for j, a in enumerate((ka, va, kb, vb, ki)):
            st[j].append(a)
        h = rmsnorm(xs, norm1[l])
        qa, ka, va, qb, kb, vb, qi, ki, wi = project(h, w_in[l], pos_s, k_idx_norm_w[l], k_idx_norm_b[l])
        ki_all = jnp.concatenate([gather_pages(cache_k_idx[l], page_table), ki], axis=1)
        idx, valid = indexer_topk(qi, wi, ki_all, pos_s, topk_s)
        oa = sparse_attend(qa, gather_rows_paged(cache_k_a[l], page_table, ka, idx),
                           gather_rows_paged(cache_v_a[l], page_table, va, idx), valid)
        kb_all = jnp.concatenate([gather_pages(cache_k_b[l], page_table), kb], axis=1)
        vb_all = jnp.concatenate([gather_pages(cache_v_b[l], page_table), vb], axis=1)
        ob = diff_attend(qb, kb_all, vb_all, pos_s, lam, diff_norm_w[l], lam_init)
        xs = xs + jnp.concatenate([oa, ob], axis=-1) @ w_out[l]
        xs = xs + sq_relu_mlp(xs, norm2[l], w_up[l], w_down[l])
        for j, a in enumerate((ka, va, kb, vb, ki)):
            st[5 + j].append(a)
    y_prompt = rmsnorm(xp, norm_f)
    y_sample = rmsnorm(xs, norm_f)
    k_a_p, v_a_p, k_b_p, v_b_p, k_idx_p, k_a_s, v_a_s, k_b_s, v_b_s, k_idx_s = [jnp.stack(s, axis=0) for s in st]
    return (y_prompt, y_sample, k_a_p, v_a_p, k_b_p, v_b_p, k_idx_p, k_a_s, v_a_s, k_b_s, v_b_s, k_idx_s)
```

```python
import functools
import math

import jax
import jax.numpy as jnp
from jax import lax
from jax.experimental import pallas as pl
from jax.experimental.pallas import tpu as pltpu

F32 = jnp.float32
BF16 = jnp.bfloat16

HEAD_DIM = 128
N_HEADS_A = 16
N_KV_A = 4
N_HEADS_B = 8
N_KV_B = 4
N_IDX_HEADS = 32
IDX_DIM = 128
TOPK_MAX = 256
ROPE_THETA = 500000.0
ROT_DIM = HEAD_DIM // 4
PAGE_SIZE = 128
EPS = 1e-6
MIX_A = N_HEADS_A * HEAD_DIM
MIX_B = N_HEADS_B * 2 * HEAD_DIM

LANES = 128
VMEM_LIMIT = 56 * 1024 * 1024
NEG = -1e30
INT_MIN = -2 ** 31

PROJ_TN = 512
COL_QA, COL_QB, COL_KA, COL_VA, COL_KB, COL_VB, COL_QI, COL_TAIL = (
    0, 2048, 4096, 4608, 5120, 6144, 7168, 11264)
ZQ_WIDTH = COL_QI
PROJ_WIDTH = COL_TAIL + PROJ_TN
N_PROJ_TILES = PROJ_WIDTH // PROJ_TN
J_KA, J_VA, J_KB, J_VB, J_QI, J_TAIL = (COL_KA // PROJ_TN, COL_VA // PROJ_TN, COL_KB // PROJ_TN,
                                        COL_VB // PROJ_TN, COL_QI // PROJ_TN, COL_TAIL // PROJ_TN)


def _cparams(sem):
    return pltpu.CompilerParams(dimension_semantics=sem, vmem_limit_bytes=VMEM_LIMIT)


def _rms_rows(x_ref, g_ref, out_ref, chunk):
    n = x_ref.shape[0] // chunk

    def body(c, carry):
        r = pl.multiple_of(c * chunk, chunk)
        x = x_ref[pl.ds(r, chunk), :].astype(F32)
        ms = jnp.mean(x * x, axis=-1, keepdims=True)
        out_ref[pl.ds(r, chunk), :] = (x * lax.rsqrt(ms + EPS) * g_ref[...]).astype(out_ref.dtype)
        return carry

    lax.fori_loop(0, n, body, 0)


def _rope(z, cos, s1, s2):
    return z * cos + pltpu.roll(z, LANES - ROT_DIM // 2, 1) * s1 + pltpu.roll(z, ROT_DIM // 2, 1) * s2


def _proj_kernel(x_ref, n1_ref, w_ref, cos_ref, s1_ref, s2_ref, lnw_ref, lnb_ref,
                 zq_ref, qi_ref, kibf_ref, ka_ref, va_ref, kb_ref, vb_ref, ki_ref, wi_ref,
                 h_scr, y_scr):
    j = pl.program_id(1)
    tm = x_ref.shape[0]

    @pl.when(j == 0)
    def _():
        _rms_rows(x_ref, n1_ref, h_scr, min(64, tm))

    acc = jnp.dot(h_scr[...], w_ref[...], preferred_element_type=F32)
    is_rope = (j <= J_KA) | (j == J_KB) | (j == J_KB + 1) | ((j >= J_QI) & (j < J_TAIL))

    @pl.when(is_rope)
    def _():
        cos, s1, s2 = cos_ref[...], s1_ref[...], s2_ref[...]
        for c in range(PROJ_TN // LANES):
            y_scr[:, c * LANES:(c + 1) * LANES] = _rope(acc[:, c * LANES:(c + 1) * LANES], cos, s1, s2)

    @pl.when(jnp.logical_not(is_rope))
    def _():
        y_scr[...] = acc

    @pl.when(j < J_QI)
    def _():
        zq_ref[...] = y_scr[...].astype(BF16)

    @pl.when(j == J_KA)
    def _():
        ka_ref[...] = y_scr[...]

    @pl.when(j == J_VA)
    def _():
        va_ref[...] = y_scr[...]

    @pl.when((j == J_KB) | (j == J_KB + 1))
    def _():
        kb_ref[...] = y_scr[...]

    @pl.when((j == J_VB) | (j == J_VB + 1))
    def _():
        vb_ref[...] = y_scr[...]

    @pl.when((j >= J_QI) & (j < J_TAIL))
    def _():
        for c in range(PROJ_TN // LANES):
            qi_ref[c] = y_scr[:, c * LANES:(c + 1) * LANES].astype(BF16)

    @pl.when(j == J_TAIL)
    def _():
        z = y_scr[:, 0:IDX_DIM]
        mu = jnp.mean(z, axis=-1, keepdims=True)
        zc = z - mu
        var = jnp.mean(zc * zc, axis=-1, keepdims=True)
        y = zc * lax.rsqrt(var + EPS) * lnw_ref[...] + lnb_ref[...]
        y = _rope(y, cos_ref[...], s1_ref[...], s2_ref[...])
        ki_ref[...] = y
        kibf_ref[...] = y.astype(BF16)
        wi_ref[...] = y_scr[:, IDX_DIM:2 * IDX_DIM] * (N_IDX_HEADS ** -0.5)


def _project(x, norm1, w_perm, tables, lnw, lnb, tm, n_pos_blocks):
    m, d = x.shape
    tn = PROJ_TN
    cos_t, s1_t, s2_t = tables
    row = lambda i, j: (i, 0)
    tab = lambda i, j: (i % n_pos_blocks, 0)
    const = lambda i, j: (0, 0)
    in_specs = [
        pl.BlockSpec((tm, d), row),
        pl.BlockSpec((1, d), const),
        pl.BlockSpec((d, tn), lambda i, j: (0, j)),
        pl.BlockSpec((tm, LANES), tab),
        pl.BlockSpec((tm, LANES), tab),
        pl.BlockSpec((tm, LANES), tab),
        pl.BlockSpec((1, LANES), const),
        pl.BlockSpec((1, LANES), const),
    ]
    out_specs = [
        pl.BlockSpec((tm, tn), lambda i, j: (i, jnp.minimum(j, J_QI - 1))),
        pl.BlockSpec((tn // LANES, tm, LANES), lambda i, j: (jnp.clip(j - J_QI, 0, J_TAIL - J_QI - 1), i, 0)),
        pl.BlockSpec((tm, LANES), row),
        pl.BlockSpec((tm, tn), row),
        pl.BlockSpec((tm, tn), row),
        pl.BlockSpec((tm, tn), lambda i, j: (i, jnp.clip(j - J_KB, 0, 1))),
        pl.BlockSpec((tm, tn), lambda i, j: (i, jnp.clip(j - J_VB, 0, 1))),
        pl.BlockSpec((tm, LANES), row),
        pl.BlockSpec((tm, LANES), row),
    ]
    out_shape = [
        jax.ShapeDtypeStruct((m, ZQ_WIDTH), BF16),
        jax.ShapeDtypeStruct((N_IDX_HEADS, m, IDX_DIM), BF16),
        jax.ShapeDtypeStruct((m, IDX_DIM), BF16),
        jax.ShapeDtypeStruct((m, N_KV_A * HEAD_DIM), F32),
        jax.ShapeDtypeStruct((m, N_KV_A * HEAD_DIM), F32),
        jax.ShapeDtypeStruct((m, N_KV_B * 2 * HEAD_DIM), F32),
        jax.ShapeDtypeStruct((m, N_KV_B * 2 * HEAD_DIM), F32),
        jax.ShapeDtypeStruct((m, IDX_DIM), F32),
        jax.ShapeDtypeStruct((m, LANES), F32),
    ]
    return pl.pallas_call(
        _proj_kernel,
        grid=(m // tm, N_PROJ_TILES),
        in_specs=in_specs,
        out_specs=out_specs,
        out_shape=out_shape,
        scratch_shapes=[pltpu.VMEM((tm, d), BF16), pltpu.VMEM((tm, tn), F32)],
        compiler_params=_cparams(("parallel", "arbitrary")),
        name="in_proj",
    )(x, norm1, w_perm, cos_t, s1_t, s2_t, lnw, lnb)


def _sort_key(score):
    b = lax.bitcast_convert_type(score, jnp.int32)
    return jnp.where(b < 0, b ^ jnp.int32(0x7FFFFFFF), b)


def _kth_largest(count_ge, k, rows):
    kf = jnp.float32(k)
    c0 = count_ge(jnp.zeros((rows, 1), jnp.int32))
    prefix = jnp.where(c0 >= kf, jnp.int32(0), jnp.int32(INT_MIN))

    def body(i, prefix):
        cand = prefix | lax.shift_left(jnp.int32(1), jnp.int32(30) - i)
        return jnp.where(count_ge(cand) >= kf, cand, prefix)

    return lax.fori_loop(0, 31, body, prefix)


def _idx_kernel(qi_ref, wi_ref, kt_ref, bias_ref, wb_scr, key_scr, *, topk, tk):
    qb = pl.program_id(1)
    tq, seq = bias_ref.shape
    nh = qi_ref.shape[0]
    row0 = qb * tq
    nkb = (row0 + tq + tk - 1) // tk

    w = wi_ref[...] * (IDX_DIM ** -0.5)
    for h in range(nh):
        wb_scr[h] = jnp.broadcast_to(w[:, h:h + 1], (tq, LANES))

    q = qi_ref[...].reshape(nh * tq, IDX_DIM)
    rows = row0 + lax.broadcasted_iota(jnp.int32, (tq, LANES), 0)
    lane = lax.broadcasted_iota(jnp.int32, (tq, LANES), 1)

    def score_block(kb, carry):
        off = pl.multiple_of(kb * tk, tk)
        lg = jnp.dot(q, kt_ref[:, pl.ds(off, tk)], preferred_element_type=F32)
        for c in range(tk // LANES):
            acc = jnp.zeros((tq, LANES), F32)
            for h in range(nh):
                acc = acc + jnp.maximum(lg[h * tq:(h + 1) * tq, c * LANES:(c + 1) * LANES], 0.0) * wb_scr[h]
            cols = off + c * LANES + lane
            key_scr[:, pl.ds(off + c * LANES, LANES)] = jnp.where(cols <= rows, _sort_key(acc), jnp.int32(INT_MIN))
        return carry

    lax.fori_loop(0, nkb, score_block, 0)

    nch = nkb * (tk // LANES)

    def count_ge(cand):
        candb = jnp.broadcast_to(cand, (tq, LANES))

        def body(c, acc):
            kk = key_scr[:, pl.ds(pl.multiple_of(c * LANES, LANES), LANES)]
            return acc + jnp.where(kk >= candb, 1.0, 0.0)

        acc = lax.fori_loop(0, nch, body, jnp.zeros((tq, LANES), F32))
        return jnp.sum(acc, axis=1, keepdims=True)

    thr = jnp.broadcast_to(_kth_largest(count_ge, topk, tq), (tq, LANES))

    def write_bias(c, carry):
        off = pl.multiple_of(c * LANES, LANES)
        kk = key_scr[:, pl.ds(off, LANES)]
        sel = (kk >= thr) & ((off + lane) <= rows)
        bias_ref[:, pl.ds(off, LANES)] = jnp.where(sel, 0.0, NEG)
        return carry

    lax.fori_loop(0, nch, write_bias, 0)

    def fill_neg(c, carry):
        bias_ref[:, pl.ds(pl.multiple_of(c * LANES, LANES), LANES)] = jnp.full((tq, LANES), NEG, F32)
        return carry

    lax.fori_loop(nch, seq // LANES, fill_neg, 0)


def _prompt_indexer(qi, wi, kit, batch, seq, topk, tq, tk):
    nq = seq // tq
    return pl.pallas_call(
        functools.partial(_idx_kernel, topk=topk, tk=tk),
        grid=(batch, nq),
        in_specs=[
            pl.BlockSpec((N_IDX_HEADS, tq, IDX_DIM), lambda b, q: (0, b * nq + q, 0)),
            pl.BlockSpec((tq, LANES), lambda b, q: (b * nq + q, 0)),
            pl.BlockSpec((None, IDX_DIM, seq), lambda b, q: (b, 0, 0)),
        ],
        out_specs=pl.BlockSpec((tq, seq), lambda b, q: (b * nq + q, 0)),
        out_shape=jax.ShapeDtypeStruct((batch * seq, seq), F32),
        scratch_shapes=[pltpu.VMEM((N_IDX_HEADS, tq, LANES), F32), pltpu.VMEM((tq, seq), jnp.int32)],
        compiler_params=_cparams(("parallel", "arbitrary")),
        name="prompt_indexer",
    )(qi, wi, kit)


def _online_softmax_step(s, m, l):
    m_new = jnp.maximum(m, jnp.max(s, axis=1, keepdims=True))
    p = jnp.exp(s - m_new)
    alpha = jnp.exp(m - m_new)
    return p, alpha, m_new, alpha * l + jnp.sum(p, axis=1, keepdims=True)


def _dsa_kernel(q_ref, kt_ref, v_ref, bias_ref, o_ref, acc_scr, *, tk):
    qb = pl.program_id(2)
    tq = q_ref.shape[0]
    g_a = q_ref.shape[1] // HEAD_DIM
    nkb = (qb * tq + tq + tk - 1) // tk
    q = jnp.concatenate([q_ref[:, r * HEAD_DIM:(r + 1) * HEAD_DIM] for r in range(g_a)], axis=0)
    acc_scr[...] = jnp.zeros(acc_scr.shape, F32)

    def body(kb, carry):
        m, l = carry
        off = pl.multiple_of(kb * tk, tk)
        s = jnp.dot(q, kt_ref[:, pl.ds(off, tk)], preferred_element_type=F32) * (HEAD_DIM ** -0.5)
        b = bias_ref[:, pl.ds(off, tk)]
        s = s + jnp.concatenate([b] * g_a, axis=0)
        p, alpha, m, l = _online_softmax_step(s, m, l)
        pv = jnp.dot(p.astype(BF16), v_ref[pl.ds(off, tk), :], preferred_element_type=F32)
        acc_scr[...] = alpha * acc_scr[...] + pv
        return m, l

    init = (jnp.full((g_a * tq, 1), NEG, F32), jnp.zeros((g_a * tq, 1), F32))
    m, l = lax.fori_loop(0, nkb, body, init)
    out = acc_scr[...] / l
    for r in range(g_a):
        o_ref[:, r * HEAD_DIM:(r + 1) * HEAD_DIM] = out[r * tq:(r + 1) * tq].astype(o_ref.dtype)


def _dsa_prompt(zq, kat, bias, batch, seq, tq, tk):
    nq = seq // tq
    gw = (N_HEADS_A // N_KV_A) * HEAD_DIM
    return pl.pallas_call(
        functools.partial(_dsa_kernel, tk=tk),
        grid=(batch, N_KV_A, nq),
        in_specs=[
            pl.BlockSpec((tq, gw), lambda b, g, q: (b * nq + q, COL_QA // gw + g)),
            pl.BlockSpec((None, None, HEAD_DIM, seq), lambda b, g, q: (b, g, 0, 0)),
            pl.BlockSpec((seq, HEAD_DIM), lambda b, g, q: (b, COL_VA // HEAD_DIM + g)),
            pl.BlockSpec((tq, seq), lambda b, g, q: (b * nq + q, 0)),
        ],
        out_specs=pl.BlockSpec((tq, gw), lambda b, g, q: (b * nq + q, g)),
        out_shape=jax.ShapeDtypeStruct((batch * seq, MIX_A), BF16),
        scratch_shapes=[pltpu.VMEM(((N_HEADS_A // N_KV_A) * tq, HEAD_DIM), F32)],
        compiler_params=_cparams(("parallel", "parallel", "arbitrary")),
        name="dsa_prompt",
    )(zq, kat, zq, bias)


def _lambda(lq1, lk1, lq2, lk2, lam_init):
    return (jnp.exp(jnp.sum(lq1[...] * lk1[...], axis=1, keepdims=True))
            - jnp.exp(jnp.sum(lq2[...] * lk2[...], axis=1, keepdims=True)) + lam_init)


def _diff_kernel(q_ref, kt_ref, v_ref, lq1, lk1, lq2, lk2, gn_ref, o_ref, acc_scr, *, tk, lam_init):
    qb = pl.program_id(2)
    tq = q_ref.shape[0]
    g_b = q_ref.shape[1] // (2 * HEAD_DIM)
    rows_n = g_b * tq
    row0 = qb * tq
    nkb = (row0 + tq + tk - 1) // tk
    n_full = (row0 + 1) // tk
    qs = [jnp.concatenate([q_ref[:, (2 * r + c) * HEAD_DIM:(2 * r + c + 1) * HEAD_DIM] for r in range(g_b)], axis=0)
          for c in range(2)]
    acc_scr[...] = jnp.zeros(acc_scr.shape, F32)
    ri = lax.broadcasted_iota(jnp.int32, (tq, tk), 0)
    rows = row0 + jnp.concatenate([ri] * g_b, axis=0)
    cols = lax.broadcasted_iota(jnp.int32, (rows_n, tk), 1)

    def make_body(masked):
        def body(kb, carry):
            off = pl.multiple_of(kb * tk, tk)
            v = v_ref[pl.ds(off, tk), :]
            new = []
            for c in range(2):
                m, l = carry[c]
                s = jnp.dot(qs[c], kt_ref[c, :, pl.ds(off, tk)], preferred_element_type=F32) * (HEAD_DIM ** -0.5)
                if masked:
                    s = jnp.where(off + cols <= rows, s, NEG)
                p, alpha, m, l = _online_softmax_step(s, m, l)
                acc_scr[c] = alpha * acc_scr[c] + jnp.dot(p.astype(BF16), v, preferred_element_type=F32)
                new.append((m, l))
            return tuple(new)
        return body

    one = (jnp.full((rows_n, 1), NEG, F32), jnp.zeros((rows_n, 1), F32))
    carry = lax.fori_loop(0, n_full, make_body(False), (one, one))
    carry = lax.fori_loop(n_full, nkb, make_body(True), carry)
    lam = _lambda(lq1, lk1, lq2, lk2, lam_init)
    o = acc_scr[0] / carry[0][1] - lam * (acc_scr[1] / carry[1][1])
    ms = jnp.mean(o * o, axis=-1, keepdims=True)
    o = o * lax.rsqrt(ms + EPS) * gn_ref[...] * (1.0 - lam_init)
    vw = 2 * HEAD_DIM
    for r in range(g_b):
        o_ref[:, r * vw:(r + 1) * vw] = o[r * tq:(r + 1) * tq].astype(o_ref.dtype)


def _diff_prompt(zq, kbt, lams, gn, batch, seq, tq, tk, lam_init):
    nq = seq // tq
    g_b = N_HEADS_B // N_KV_B
    gw = g_b * 2 * HEAD_DIM
    vw = 2 * HEAD_DIM
    vec = pl.BlockSpec((1, HEAD_DIM), lambda b, g, q: (0, 0))
    return pl.pallas_call(
        functools.partial(_diff_kernel, tk=tk, lam_init=lam_init),
        grid=(batch, N_KV_B, nq),
        in_specs=[
            pl.BlockSpec((tq, gw), lambda b, g, q: (b * nq + q, COL_QB // gw + g)),
            pl.BlockSpec((None, 2, HEAD_DIM, seq), lambda b, g, q: (b, g, 0, 0)),
            pl.BlockSpec((seq, vw), lambda b, g, q: (b, COL_VB // vw + g)),
            vec, vec, vec, vec,
            pl.BlockSpec((1, vw), lambda b, g, q: (0, 0)),
        ],
        out_specs=pl.BlockSpec((tq, gw), lambda b, g, q: (b * nq + q, g)),
        out_shape=jax.ShapeDtypeStruct((batch * seq, MIX_B), BF16),
        scratch_shapes=[pltpu.VMEM((2, g_b * tq, vw), F32)],
        compiler_params=_cparams(("parallel", "parallel", "arbitrary")),
        name="diff_prompt",
    )(zq, kbt, zq, *lams, gn)


def _out_proj_kernel(a_ref, b_ref, w_ref, x_ref, o_ref):
    ka = a_ref.shape[1]
    acc = jnp.dot(a_ref[...], w_ref[0:ka, :], preferred_element_type=F32)
    acc = acc + jnp.dot(b_ref[...], w_ref[ka:, :], preferred_element_type=F32)
    o_ref[...] = x_ref[...] + acc


def _out_proj(oa, ob, w, x, tm, tn):
    m, n = x.shape
    return pl.pallas_call(
        _out_proj_kernel,
        grid=(n // tn, m // tm),
        in_specs=[
            pl.BlockSpec((tm, oa.shape[1]), lambda j, i: (i, 0)),
            pl.BlockSpec((tm, ob.shape[1]), lambda j, i: (i, 0)),
            pl.BlockSpec((w.shape[0], tn), lambda j, i: (0, j)),
            pl.BlockSpec((tm, tn), lambda j, i: (i, j)),
        ],
        out_specs=pl.BlockSpec((tm, tn), lambda j, i: (i, j)),
        out_shape=jax.ShapeDtypeStruct((m, n), F32),
        compiler_params=_cparams(("parallel", "parallel")),
        name="out_proj",
    )(oa, ob, w, x)


def _mlp_up_kernel(x_ref, g_ref, w_ref, u_ref, h_scr):
    @pl.when(pl.program_id(1) == 0)
    def _():
        _rms_rows(x_ref, g_ref, h_scr, min(64, x_ref.shape[0]))

    a = jnp.maximum(jnp.dot(h_scr[...], w_ref[...], preferred_element_type=F32), 0.0)
    u_ref[...] = (a * a).astype(u_ref.dtype)


def _mlp_up(x, g, w, tm, tn):
    m, d = x.shape
    f = w.shape[1]
    return pl.pallas_call(
        _mlp_up_kernel,
        grid=(m // tm, f // tn),
        in_specs=[
            pl.BlockSpec((tm, d), lambda i, j: (i, 0)),
            pl.BlockSpec((1, d), lambda i, j: (0, 0)),
            pl.BlockSpec((d, tn), lambda i, j: (0, j)),
        ],
        out_specs=pl.BlockSpec((tm, tn), lambda i, j: (i, j)),
        out_shape=jax.ShapeDtypeStruct((m, f), BF16),
        scratch_shapes=[pltpu.VMEM((tm, d), BF16)],
        compiler_params=_cparams(("parallel", "arbitrary")),
        name="mlp_up",
    )(x, g, w)


def _mlp_down_kernel(u_ref, w_ref, x_ref, g_ref, o_ref, *, final_norm):
    k = pl.program_id(1)
    d = jnp.dot(u_ref[...], w_ref[...], preferred_element_type=F32)

    @pl.when(k == 0)
    def _():
        o_ref[...] = x_ref[...] + d

    @pl.when(k > 0)
    def _():
        o_ref[...] = o_ref[...] + d

    if final_norm:
        @pl.when(k == pl.num_programs(1) - 1)
        def _():
            _rms_rows(o_ref, g_ref, o_ref, min(64, o_ref.shape[0]))


def _mlp_down(u, w, x, g, tm, tk, final_norm):
    m, d = x.shape
    f = u.shape[1]
    return pl.pallas_call(
        functools.partial(_mlp_down_kernel, final_norm=final_norm),
        grid=(m // tm, f // tk),
        in_specs=[
            pl.BlockSpec((tm, tk), lambda i, k: (i, k)),
            pl.BlockSpec((tk, d), lambda i, k: (k, 0)),
            pl.BlockSpec((tm, d), lambda i, k: (i, 0)),
            pl.BlockSpec((1, d), lambda i, k: (0, 0)),
        ],
        out_specs=pl.BlockSpec((tm, d), lambda i, k: (i, 0)),
        out_shape=jax.ShapeDtypeStruct((m, d), F32),
        compiler_params=_cparams(("parallel", "arbitrary")),
        name="mlp_down",
    )(u, w, x, g)


def _nt_dot(a, b):
    return lax.dot_general(a, b, (((1,), (1,)), ((), ())), preferred_element_type=F32)


def _sidx_kernel(pt_ref, q_ref, w_ref, knew_ref, *rest):
    n_pages = len(rest) - 2
    page_refs, sc_ref, self_ref = rest[:n_pages], rest[n_pages], rest[n_pages + 1]
    q = q_ref[...]
    w = w_ref[...] * (IDX_DIM ** -0.5)
    for r in range(n_pages):
        lg = _nt_dot(q, page_refs[r][...].astype(BF16))
        sc_ref[:, r * PAGE_SIZE:(r + 1) * PAGE_SIZE] = jnp.sum(jnp.maximum(lg, 0.0) * w, axis=0, keepdims=True)
    kn = knew_ref[...].astype(BF16).astype(F32)
    lg = jnp.sum(q.astype(F32) * kn, axis=1, keepdims=True)
    self_ref[...] = jnp.broadcast_to(jnp.sum(jnp.maximum(lg, 0.0) * w, axis=0, keepdims=True), self_ref.shape)


def _sample_scores(page_table, qi_s, wcol, ki_new, cache_k_idx_l):
    db, n_pages = page_table.shape
    page_specs = [pl.BlockSpec((None, PAGE_SIZE, IDX_DIM), functools.partial(lambda b, pt, r: (pt[b, r], 0, 0), r=r))
                  for r in range(n_pages)]
    return pl.pallas_call(
        _sidx_kernel,
        grid_spec=pltpu.PrefetchScalarGridSpec(
            num_scalar_prefetch=1,
            grid=(db,),
            in_specs=[
                pl.BlockSpec((None, N_IDX_HEADS, IDX_DIM), lambda b, pt: (b, 0, 0)),
                pl.BlockSpec((None, N_IDX_HEADS, 1), lambda b, pt: (b, 0, 0)),
                pl.BlockSpec((None, 1, IDX_DIM), lambda b, pt: (b, 0, 0)),
            ] + page_specs,
            out_specs=[
                pl.BlockSpec((None, 1, n_pages * PAGE_SIZE), lambda b, pt: (b, 0, 0)),
                pl.BlockSpec((None, 1, LANES), lambda b, pt: (b, 0, 0)),
            ],
        ),
        out_shape=[jax.ShapeDtypeStruct((db, 1, n_pages * PAGE_SIZE), F32),
                   jax.ShapeDtypeStruct((db, 1, LANES), F32)],
        compiler_params=_cparams(("parallel",)),
        name="sample_scores",
    )(page_table, qi_s, wcol, ki_new, *([cache_k_idx_l] * n_pages))


def _stopk_kernel(sc_ref, self_ref, bias_ref, bself_ref, *, topk):
    db, past = sc_ref.shape
    keys = _sort_key(sc_ref[...])
    kself = _sort_key(self_ref[...])

    def count_ge(cand):
        c = jnp.sum(jnp.where(keys >= cand, 1.0, 0.0), axis=1, keepdims=True)
        return c + jnp.where(kself[:, 0:1] >= cand, 1.0, 0.0)

    thr = _kth_largest(count_ge, topk, db)
    bias_ref[...] = jnp.where(keys >= thr, 0.0, NEG)
    bself_ref[...] = jnp.where(kself >= thr, 0.0, NEG)


def _sample_topk(scores, self_scores, topk):
    db, past = scores.shape
    return pl.pallas_call(
        functools.partial(_stopk_kernel, topk=topk),
        out_shape=[jax.ShapeDtypeStruct((db, past), F32), jax.ShapeDtypeStruct((db, LANES), F32)],
        compiler_params=pltpu.CompilerParams(vmem_limit_bytes=VMEM_LIMIT),
        name="sample_topk",
    )(scores, self_scores)


def _sattn_kernel(pt_ref, qa_ref, qb_ref, bias_ref, bself_ref, kas_ref, vas_ref, kbs_ref, vbs_ref,
                  lq1, lk1, lq2, lk2, gn_ref, *rest, pg, lam_init):
    ka_refs, va_refs, kb_refs, vb_refs = rest[0:pg], rest[pg:2 * pg], rest[2 * pg:3 * pg], rest[3 * pg:4 * pg]
    oa_ref, ob_ref, ma, la, acca, mb, lb, accb = rest[4 * pg:]
    step = pl.program_id(1)
    nh = N_HEADS_A
    scale = HEAD_DIM ** -0.5

    @pl.when(step == 0)
    def _():
        ma[...] = jnp.full(ma.shape, NEG, F32)
        mb[...] = jnp.full(mb.shape, NEG, F32)
        la[...] = jnp.zeros(la.shape, F32)
        lb[...] = jnp.zeros(lb.shape, F32)
        acca[...] = jnp.zeros(acca.shape, F32)
        accb[...] = jnp.zeros(accb.shape, F32)

    qa = qa_ref[...]
    qb = qb_ref[...]
    row = lax.broadcasted_iota(jnp.int32, (nh, LANES), 0)
    grp_a = row // (N_HEADS_A // N_KV_A)
    grp_b = (row % N_HEADS_B) // (N_HEADS_B // N_KV_B)
    cmp_b = row // N_HEADS_B

    sa, sb = [], []
    for r in range(pg):
        s = jnp.zeros((nh, PAGE_SIZE), F32)
        for g in range(N_KV_A):
            kg = ka_refs[r][pl.ds(g, PAGE_SIZE, stride=N_KV_A), :].astype(BF16)
            s = jnp.where(grp_a == g, _nt_dot(qa, kg), s)
        sa.append(s * scale + bias_ref[:, r * PAGE_SIZE:(r + 1) * PAGE_SIZE])
        s = jnp.zeros((nh, PAGE_SIZE), F32)
        for g in range(N_KV_B):
            for c in range(2):
                kg = kb_refs[r][pl.ds(g * 2 + c, PAGE_SIZE, stride=2 * N_KV_B), :].astype(BF16)
                s = jnp.where((grp_b == g) & (cmp_b == c), _nt_dot(qb, kg), s)
        sb.append(s * scale)
    sa = jnp.concatenate(sa, axis=1)
    sb = jnp.concatenate(sb, axis=1)

    pa, alpha_a, m_new, l_new = _online_softmax_step(sa, ma[...], la[...])
    ma[...] = m_new
    la[...] = l_new
    pb, alpha_b, m_new, l_new = _online_softmax_step(sb, mb[...], lb[...])
    mb[...] = m_new
    lb[...] = l_new
    pa = pa.astype(BF16)
    pb = pb.astype(BF16)
    rowv = lax.broadcasted_iota(jnp.int32, (nh, 2 * HEAD_DIM), 0)
    grp_bv = (rowv % N_HEADS_B) // (N_HEADS_B // N_KV_B)
    oa = jnp.zeros((nh, HEAD_DIM), F32)
    ob = jnp.zeros((nh, 2 * HEAD_DIM), F32)
    for r in range(pg):
        par = pa[:, r * PAGE_SIZE:(r + 1) * PAGE_SIZE]
        pbr = pb[:, r * PAGE_SIZE:(r + 1) * PAGE_SIZE]
        ta = jnp.zeros((nh, HEAD_DIM), F32)
        tb = jnp.zeros((nh, 2 * HEAD_DIM), F32)
        for g in range(N_KV_A):
            vg = va_refs[r][pl.ds(g, PAGE_SIZE, stride=N_KV_A), :].astype(BF16)
            ta = jnp.where(grp_a == g, jnp.dot(par, vg, preferred_element_type=F32), ta)
        for g in range(N_KV_B):
            vg = vb_refs[r][:, g, :].astype(BF16)
            tb = jnp.where(grp_bv == g, jnp.dot(pbr, vg, preferred_element_type=F32), tb)
        oa = oa + ta
        ob = ob + tb
    acca[...] = alpha_a * acca[...] + oa
    accb[...] = alpha_b * accb[...] + ob

    @pl.when(step == pl.num_programs(1) - 1)
    def _():
        rnd = lambda ref: ref[...].astype(BF16).astype(F32)
        s_self = jnp.sum(qa.astype(F32) * rnd(kas_ref), axis=1, keepdims=True) * scale + bself_ref[:, 0:1]
        m_new = jnp.maximum(ma[...], s_self)
        p = jnp.exp(s_self - m_new)
        alpha = jnp.exp(ma[...] - m_new)
        l = alpha * la[...] + p
        o = alpha * acca[...] + p.astype(BF16).astype(F32) * rnd(vas_ref)
        oa_ref[...] = (o / l).astype(oa_ref.dtype)

        s_self = jnp.sum(qb.astype(F32) * rnd(kbs_ref), axis=1, keepdims=True) * scale
        m_new = jnp.maximum(mb[...], s_self)
        p = jnp.exp(s_self - m_new)
        alpha = jnp.exp(mb[...] - m_new)
        l = alpha * lb[...] + p
        o = (alpha * accb[...] + p.astype(BF16).astype(F32) * rnd(vbs_ref)) / l
        lam = _lambda(lq1, lk1, lq2, lk2, lam_init)
        o = o[0:N_HEADS_B] - lam * o[N_HEADS_B:]
        ms = jnp.mean(o * o, axis=-1, keepdims=True)
        ob_ref[...] = (o * lax.rsqrt(ms + EPS) * gn_ref[...] * (1.0 - lam_init)).astype(ob_ref.dtype)


def _sample_attention(page_table, qa_s, qb_s, bias, bself, kas, vas, kbs, vbs, lams, gn,
                      ka2d, va2d, kb2d, vb_l, pg, lam_init):
    db, n_pages = page_table.shape
    nh = N_HEADS_A
    vw = 2 * HEAD_DIM

    def page_spec(shape, r):
        nd = len(shape)
        return pl.BlockSpec(shape, functools.partial(
            lambda b, s, pt, r: (pt[b, s * pg + r],) + (0,) * (nd - 1), r=r))

    per_b = lambda shape: pl.BlockSpec((None,) + shape, lambda b, s, pt: (b,) + (0,) * len(shape))
    vec = pl.BlockSpec((1, HEAD_DIM), lambda b, s, pt: (0, 0))
    in_specs = [
        per_b((nh, HEAD_DIM)), per_b((nh, HEAD_DIM)),
        pl.BlockSpec((None, 1, pg * PAGE_SIZE), lambda b, s, pt: (b, 0, s)),
        per_b((1, LANES)),
        per_b((nh, HEAD_DIM)), per_b((nh, HEAD_DIM)), per_b((nh, HEAD_DIM)), per_b((nh, vw)),
        vec, vec, vec, vec,
        pl.BlockSpec((1, vw), lambda b, s, pt: (0, 0)),
    ]
    in_specs += [page_spec((PAGE_SIZE * N_KV_A, HEAD_DIM), r) for r in range(pg)]
    in_specs += [page_spec((PAGE_SIZE * N_KV_A, HEAD_DIM), r) for r in range(pg)]
    in_specs += [page_spec((PAGE_SIZE * N_KV_B * 2, HEAD_DIM), r) for r in range(pg)]
    in_specs += [pl.BlockSpec((None, PAGE_SIZE, N_KV_B, vw), functools.partial(
        lambda b, s, pt, r: (pt[b, s * pg + r], 0, 0, 0), r=r)) for r in range(pg)]
    return pl.pallas_call(
        functools.partial(_sattn_kernel, pg=pg, lam_init=lam_init),
        grid_spec=pltpu.PrefetchScalarGridSpec(
            num_scalar_prefetch=1,
            grid=(db, n_pages // pg),
            in_specs=in_specs,
            out_specs=[per_b((nh, HEAD_DIM)), per_b((N_HEADS_B, vw))],
            scratch_shapes=[
                pltpu.VMEM((nh, 1), F32), pltpu.VMEM((nh, 1), F32), pltpu.VMEM((nh, HEAD_DIM), F32),
                pltpu.VMEM((nh, 1), F32), pltpu.VMEM((nh, 1), F32), pltpu.VMEM((nh, vw), F32),
            ],
        ),
        out_shape=[jax.ShapeDtypeStruct((db, nh, HEAD_DIM), BF16),
                   jax.ShapeDtypeStruct((db, N_HEADS_B, vw), BF16)],
        compiler_params=_cparams(("parallel", "arbitrary")),
        name="sample_attention",
    )(page_table, qa_s, qb_s, bias, bself, kas, vas, kbs, vbs, *lams, gn,
      *([ka2d] * pg), *([va2d] * pg), *([kb2d] * pg), *([vb_l] * pg))


def _rope_tables(pos):
    half = ROT_DIM // 2
    inv = jnp.power(ROPE_THETA, -jnp.arange(half, dtype=F32) / half)
    ang = pos.astype(F32)[:, None] * inv[None, :]
    cos, sin = jnp.cos(ang), jnp.sin(ang)
    n = pos.shape[0]
    z = lambda w: jnp.zeros((n, w), F32)
    cos_t = jnp.concatenate([cos, cos, jnp.ones((n, HEAD_DIM - ROT_DIM), F32)], axis=1)
    s1_t = jnp.concatenate([-sin, z(HEAD_DIM - half)], axis=1)
    s2_t = jnp.concatenate([z(half), sin, z(HEAD_DIM - ROT_DIM)], axis=1)
    return cos_t, s1_t, s2_t


def _permute_w_in(w):
    widths = (MIX_A, N_KV_A * HEAD_DIM, N_KV_A * HEAD_DIM, MIX_B, N_KV_B * 2 * HEAD_DIM, N_KV_B * 2 * HEAD_DIM,
              N_IDX_HEADS * IDX_DIM, IDX_DIM, N_IDX_HEADS)
    offs = [0]
    for wd in widths:
        offs.append(offs[-1] + wd)
    seg = [w[:, offs[i]:offs[i + 1]] for i in range(len(widths))]
    q_a, k_a, v_a, q_b, k_b, v_b, q_i, k_i, w_i = seg
    pad = jnp.zeros((w.shape[0], PROJ_WIDTH - offs[-1]), w.dtype)
    return jnp.concatenate([q_a, q_b, k_a, v_a, k_b, v_b, q_i, k_i, w_i, pad], axis=1).astype(BF16)


def _tile(n, pref):
    return pref if n % pref == 0 else n


def kernel(x_prompt, x_sample, cache_k_a, cache_v_a, cache_k_b, cache_v_b, cache_k_idx, page_table, norm1, w_in, k_idx_norm_w, k_idx_norm_b, lambda_q1, lambda_k1, lambda_q2, lambda_k2, diff_norm_w, w_out, norm2, w_up, w_down, norm_f):
    batch, seq, d_model = x_prompt.shape
    db, dec_seq, _ = x_sample.shape
    assert dec_seq == 1
    depth = w_in.shape[0]
    n_pages = page_table.shape[1]
    past = n_pages * PAGE_SIZE
    n_pool = cache_k_a.shape[1]
    topk_p = min(TOPK_MAX, seq // 4)
    topk_s = min(TOPK_MAX, (past + dec_seq) // 4)

    mp = batch * seq
    tm_p = _tile(mp, 512)
    tq_a = _tile(seq, 128)
    tq_b = _tile(seq, 256)
    tk_att = _tile(seq, 512)
    tk_idx = _tile(seq, 256)
    pg = 8 if n_pages % 8 == 0 else n_pages

    tab_p = _rope_tables(jnp.arange(seq))
    tab_s = _rope_tables(jnp.full((db,), past, jnp.int32))
    row = lambda a: a.reshape(1, -1)

    xp = x_prompt.reshape(mp, d_model)
    xs = x_sample.reshape(db, d_model)
    st = [[] for _ in range(10)]
    for l in range(depth):
        lam_init = 0.8 - 0.6 * math.exp(-0.3 * l)
        w_in_p = _permute_w_in(w_in[l])
        w_out_b = w_out[l].astype(BF16)
        w_up_b = w_up[l].astype(BF16)
        w_down_b = w_down[l].astype(BF16)
        lams = (row(lambda_q1[l]), row(lambda_k1[l]), row(lambda_q2[l]), row(lambda_k2[l]))
        gn = row(diff_norm_w[l])
        n1, n2 = row(norm1[l]), row(norm2[l])
        lnw, lnb = row(k_idx_norm_w[l]), row(k_idx_norm_b[l])

        zq, qi, kibf, ka, va, kb, vb, ki, wi = _project(xp, n1, w_in_p, tab_p, lnw, lnb, tm_p, seq // tm_p)
        kit = kibf.reshape(batch, seq, IDX_DIM).transpose(0, 2, 1)
        kat = zq[:, COL_KA:COL_VA].reshape(batch, seq, N_KV_A, HEAD_DIM).transpose(0, 2, 3, 1)
        kbt = zq[:, COL_KB:COL_VB].reshape(batch, seq, N_KV_B * 2, HEAD_DIM).transpose(0, 2, 3, 1)
        bias = _prompt_indexer(qi, wi, kit, batch, seq, topk_p, tq_a, tk_idx)
        oa = _dsa_prompt(zq, kat, bias, batch, seq, tq_a, tk_att)
        ob = _diff_prompt(zq, kbt, lams, gn, batch, seq, tq_b, tk_att, lam_init)
        xp = _out_proj(oa, ob, w_out_b, xp, tm_p, 1024)
        u = _mlp_up(xp, n2, w_up_b, tm_p, 1024)
        xp = _mlp_down(u, w_down_b, xp, row(norm_f), tm_p, 512, l == depth - 1)
        st[0].append(ka.reshape(batch, seq, N_KV_A, HEAD_DIM))
        st[1].append(va.reshape(batch, seq, N_KV_A, HEAD_DIM))
        st[2].append(kb.reshape(batch, seq, N_KV_B, 2, HEAD_DIM))
        st[3].append(vb.reshape(batch, seq, N_KV_B, 2 * HEAD_DIM))
        st[4].append(ki.reshape(batch, seq, IDX_DIM))

        zq_s, qi_s, _, ka_s, va_s, kb_s, vb_s, ki_s, wi_s = _project(xs, n1, w_in_p, tab_s, lnw, lnb, db, 1)
        qi_sb = qi_s.transpose(1, 0, 2)
        wcol = wi_s[:, :N_IDX_HEADS].reshape(db, N_IDX_HEADS, 1)
        scores, self_sc = _sample_scores(page_table, qi_sb, wcol, ki_s.reshape(db, 1, IDX_DIM), cache_k_idx[l])
        bias_s, bself = _sample_topk(scores.reshape(db, past), self_sc.reshape(db, LANES), topk_s)
        qa_s = zq_s[:, COL_QA:COL_QB].reshape(db, N_HEADS_A, HEAD_DIM)
        qb_s = zq_s[:, COL_QB:COL_KA].reshape(db, N_HEADS_B, 2, HEAD_DIM).transpose(0, 2, 1, 3).reshape(
            db, 2 * N_HEADS_B, HEAD_DIM)
        rows_a = jnp.arange(N_HEADS_A) // (N_HEADS_A // N_KV_A)
        rr = jnp.arange(2 * N_HEADS_B)
        grp_b = (rr % N_HEADS_B) // (N_HEADS_B // N_KV_B)
        rows_kb = grp_b * 2 + rr // N_HEADS_B
        kas = ka_s.reshape(db, N_KV_A, HEAD_DIM)[:, rows_a]
        vas = va_s.reshape(db, N_KV_A, HEAD_DIM)[:, rows_a]
        kbs = kb_s.reshape(db, N_KV_B * 2, HEAD_DIM)[:, rows_kb]
        vbs = vb_s.reshape(db, N_KV_B, 2 * HEAD_DIM)[:, grp_b]
        oa_s, ob_s = _sample_attention(
            page_table, qa_s, qb_s, bias_s.reshape(db, 1, past), bself.reshape(db, 1, LANES), kas, vas, kbs, vbs,
            lams, gn,
            cache_k_a[l].reshape(n_pool * PAGE_SIZE * N_KV_A, HEAD_DIM),
            cache_v_a[l].reshape(n_pool * PAGE_SIZE * N_KV_A, HEAD_DIM),
            cache_k_b[l].reshape(n_pool * PAGE_SIZE * N_KV_B * 2, HEAD_DIM),
            cache_v_b[l], pg, lam_init)
        xs = _out_proj(oa_s.reshape(db, MIX_A), ob_s.reshape(db, MIX_B), w_out_b, xs, db, 1024)
        u_s = _mlp_up(xs, n2, w_up_b, db, 1024)
        st[5].append(ka_s.reshape(db, dec_seq, N_KV_A, HEAD_DIM))
        st[6].append(va_s.reshape(db, dec_seq, N_KV_A, HEAD_DIM))
        st[7].append(kb_s.reshape(db, dec_seq, N_KV_B, 2, HEAD_DIM))
        st[8].append(vb_s.reshape(db, dec_seq, N_KV_B, 2 * HEAD_DIM))
        xs = _mlp_down(u_s, w_down_b, xs, row(norm_f), db, 512, l == depth - 1)
        st[9].append(ki_s.reshape(db, dec_seq, IDX_DIM))
    outs = [jnp.stack(s, axis=0) for s in st]
    return (xp.reshape(batch, seq, d_model), xs.reshape(db, dec_seq, d_model), *outs)
```

```python
import functools
import math

import jax
import jax.numpy as jnp
from jax import lax
from jax.experimental import pallas as pl
from jax.experimental.pallas import tpu as pltpu

F32 = jnp.float32
BF16 = jnp.bfloat16

HEAD_DIM = 128
N_HEADS_A = 16
N_KV_A = 4
N_HEADS_B = 8
N_KV_B = 4
N_IDX_HEADS = 32
IDX_DIM = 128
TOPK_MAX = 256
ROPE_THETA = 500000.0
ROT_DIM = HEAD_DIM // 4
PAGE_SIZE = 128
EPS = 1e-6
MIX_A = N_HEADS_A * HEAD_DIM
MIX_B = N_HEADS_B * 2 * HEAD_DIM

LANES = 128
VMEM_LIMIT = 56 * 1024 * 1024
NEG = -1e30
Q_SCALE = HEAD_DIM ** -0.5 * math.log2(math.e)
INT_MIN = -2 ** 31

PROJ_TN = 512
COL_QA, COL_KA, COL_VA, COL_QB, COL_KB, COL_VB, COL_QI, COL_TAIL = (
    0, 2048, 2560, 3072, 5120, 6144, 7168, 11264)
N_PROJ_TILES = COL_TAIL // PROJ_TN
TAIL_WIDTH = 2 * LANES
J_KA, J_VA, J_KB, J_VB, J_QI = (COL_KA // PROJ_TN, COL_VA // PROJ_TN, COL_KB // PROJ_TN,
                                COL_VB // PROJ_TN, COL_QI // PROJ_TN)
KV32_TILES = 7


def _cparams(sem):
    return pltpu.CompilerParams(dimension_semantics=sem, vmem_limit_bytes=VMEM_LIMIT)


def _rms_rows(x_ref, g_ref, out_ref, chunk):
    n = x_ref.shape[0] // chunk

    def body(c, carry):
        r = pl.multiple_of(c * chunk, chunk)
        x = x_ref[pl.ds(r, chunk), :].astype(F32)
        ms = jnp.mean(x * x, axis=-1, keepdims=True)
        out_ref[pl.ds(r, chunk), :] = (x * lax.rsqrt(ms + EPS) * g_ref[...]).astype(out_ref.dtype)
        return carry

    lax.fori_loop(0, n, body, 0)


def _rope(z, cos, s1, s2):
    return z * cos + pltpu.roll(z, LANES - ROT_DIM // 2, 1) * s1 + pltpu.roll(z, ROT_DIM // 2, 1) * s2


def _proj_kernel(x_ref, n1_ref, w_ref, cos_ref, s1_ref, s2_ref, zq_ref, kv_ref, h_scr):
    @pl.when(pl.program_id(1) == 0)
    def _():
        _rms_rows(x_ref, n1_ref, h_scr, min(64, x_ref.shape[0]))

    acc = jnp.dot(h_scr[...], w_ref[...], preferred_element_type=F32)
    cos, s1, s2 = cos_ref[...], s1_ref[...], s2_ref[...]
    for c in range(PROJ_TN // LANES):
        y = _rope(acc[:, c * LANES:(c + 1) * LANES], cos, s1, s2)
        zq_ref[:, c * LANES:(c + 1) * LANES] = y.astype(BF16)
        kv_ref[:, c * LANES:(c + 1) * LANES] = y


def _kv32_tile(j):
    kv_tiles = (J_KA, J_VA, J_KB, J_KB + 1, J_VB, J_VB + 1)
    return sum((j > t).astype(jnp.int32) for t in kv_tiles)


def _project(x, norm1, w_b, tables, tm, n_pos_blocks):
    m, d = x.shape
    tn = PROJ_TN

    def kind(j):
        is_value = (j == J_VA) | (j == J_VB) | (j == J_VB + 1)
        is_query = (j < J_KA) | ((j > J_VA) & (j < J_KB))
        return jnp.where(is_value, 1, jnp.where(is_query, 2, 0))

    tab = pl.BlockSpec((None, tm, LANES), lambda i, j: (kind(j), i % n_pos_blocks, 0))
    return pl.pallas_call(
        _proj_kernel,
        grid=(m // tm, N_PROJ_TILES),
        in_specs=[
            pl.BlockSpec((tm, d), lambda i, j: (i, 0)),
            pl.BlockSpec((1, d), lambda i, j: (0, 0)),
            pl.BlockSpec((d, tn), lambda i, j: (0, j)),
            tab, tab, tab,
        ],
        out_specs=[
            pl.BlockSpec((tm, tn), lambda i, j: (i, j)),
            pl.BlockSpec((tm, tn), lambda i, j: (i, _kv32_tile(j))),
        ],
        out_shape=[
            jax.ShapeDtypeStruct((m, COL_TAIL), BF16),
            jax.ShapeDtypeStruct((m, KV32_TILES * tn), F32),
        ],
        scratch_shapes=[pltpu.VMEM((tm, d), BF16)],
        compiler_params=_cparams(("parallel", "arbitrary")),
        name="in_proj",
    )(x, norm1, w_b, *tables)


def _tail_kernel(x_ref, n1_ref, w_ref, cos_ref, s1_ref, s2_ref, lnw_ref, lnb_ref, ki_ref, kibf_ref, wi_ref, h_scr):
    _rms_rows(x_ref, n1_ref, h_scr, min(64, x_ref.shape[0]))
    acc = jnp.dot(h_scr[...], w_ref[...], preferred_element_type=F32)
    z = acc[:, 0:IDX_DIM]
    mu = jnp.mean(z, axis=-1, keepdims=True)
    zc = z - mu
    var = jnp.mean(zc * zc, axis=-1, keepdims=True)
    y = zc * lax.rsqrt(var + EPS) * lnw_ref[...] + lnb_ref[...]
    y = _rope(y, cos_ref[...], s1_ref[...], s2_ref[...])
    ki_ref[...] = y
    kibf_ref[...] = y.astype(BF16)
    wi_ref[...] = acc[:, IDX_DIM:2 * IDX_DIM] * (N_IDX_HEADS ** -0.5)


def _project_tail(x, norm1, w_tail, tables, lnw, lnb, tm, n_pos_blocks):
    m, d = x.shape
    row = pl.BlockSpec((tm, LANES), lambda i: (i, 0))
    tab = pl.BlockSpec((None, tm, LANES), lambda i: (0, i % n_pos_blocks, 0))
    vec = pl.BlockSpec((1, LANES), lambda i: (0, 0))
    return pl.pallas_call(
        _tail_kernel,
        grid=(m // tm,),
        in_specs=[
            pl.BlockSpec((tm, d), lambda i: (i, 0)),
            pl.BlockSpec((1, d), lambda i: (0, 0)),
            pl.BlockSpec((d, TAIL_WIDTH), lambda i: (0, 0)),
            tab, tab, tab, vec, vec,
        ],
        out_specs=[row, row, row],
        out_shape=[jax.ShapeDtypeStruct((m, IDX_DIM), F32), jax.ShapeDtypeStruct((m, IDX_DIM), BF16),
                   jax.ShapeDtypeStruct((m, LANES), F32)],
        scratch_shapes=[pltpu.VMEM((tm, d), BF16)],
        compiler_params=_cparams(("parallel",)),
        name="idx_tail",
    )(x, norm1, w_tail, *tables, lnw, lnb)


def _sort_key(score):
    b = lax.bitcast_convert_type(score, jnp.int32)
    return jnp.where(b < 0, b ^ jnp.int32(0x7FFFFFFF), b)


def _kth_largest(count_ge, k, rows):
    kf = jnp.float32(k)
    c0 = count_ge(jnp.zeros((rows, 1), jnp.int32))
    prefix = jnp.where(c0 >= kf, jnp.int32(0), jnp.int32(INT_MIN))

    def body(i, prefix):
        cand = prefix | lax.shift_left(jnp.int32(1), jnp.int32(30) - i)
        return jnp.where(count_ge(cand) >= kf, cand, prefix)

    return lax.fori_loop(0, 31, body, prefix)


def _idx_kernel(*refs, topk, tk, n_qparts):
    q_refs = refs[:n_qparts]
    wi_ref, kt_ref, bias_ref, q_scr, wb_scr, key_scr = refs[n_qparts:]
    qb = pl.program_id(1)
    tq, seq = bias_ref.shape
    nh = N_IDX_HEADS
    heads_per_part = nh // n_qparts
    row0 = qb * tq
    nkb = (row0 + tq + tk - 1) // tk
    lanes_per_kb = tk // LANES

    w = wi_ref[...] * (IDX_DIM ** -0.5)
    for h in range(nh):
        wb_scr[h] = jnp.broadcast_to(w[:, h:h + 1], (tq, LANES))
        part, hh = divmod(h, heads_per_part)
        q_scr[h * tq:(h + 1) * tq, :] = q_refs[part][:, hh * IDX_DIM:(hh + 1) * IDX_DIM]

    rows = row0 + lax.broadcasted_iota(jnp.int32, (tq, LANES), 0)
    lane = lax.broadcasted_iota(jnp.int32, (tq, LANES), 1)

    def score_block(kb, carry):
        off = pl.multiple_of(kb * tk, tk)
        lg = jnp.dot(q_scr[...], kt_ref[:, pl.ds(off, tk)], preferred_element_type=F32)
        for c in range(lanes_per_kb):
            acc = jnp.zeros((tq, LANES), F32)
            for h in range(nh):
                acc = acc + jnp.maximum(lg[h * tq:(h + 1) * tq, c * LANES:(c + 1) * LANES], 0.0) * wb_scr[h]
            cols = off + c * LANES + lane
            key_scr[:, pl.ds(off + c * LANES, LANES)] = jnp.where(cols <= rows, _sort_key(acc), jnp.int32(INT_MIN))
        return carry

    lax.fori_loop(0, nkb, score_block, 0)

    def count_ge(cand):
        candb = jnp.broadcast_to(cand, (tq, LANES))

        def body(kb, acc):
            off = pl.multiple_of(kb * tk, tk)
            for c in range(lanes_per_kb):
                acc = acc + jnp.where(key_scr[:, pl.ds(off + c * LANES, LANES)] >= candb, 1.0, 0.0)
            return acc

        acc = lax.fori_loop(0, nkb, body, jnp.zeros((tq, LANES), F32))
        return jnp.sum(acc, axis=1, keepdims=True)

    thr = jnp.broadcast_to(_kth_largest(count_ge, topk, tq), (tq, LANES))

    def write_bias(kb, carry):
        off = pl.multiple_of(kb * tk, tk)
        for c in range(lanes_per_kb):
            kk = key_scr[:, pl.ds(off + c * LANES, LANES)]
            sel = (kk >= thr) & ((off + c * LANES + lane) <= rows)
            bias_ref[:, pl.ds(off + c * LANES, LANES)] = jnp.where(sel, 0.0, NEG)
        return carry

    lax.fori_loop(0, nkb, write_bias, 0)

    def fill_neg(kb, carry):
        bias_ref[:, pl.ds(pl.multiple_of(kb * tk, tk), tk)] = jnp.full((tq, tk), NEG, F32)
        return carry

    lax.fori_loop(nkb, seq // tk, fill_neg, 0)


def _prompt_indexer(zq, wi, kit, batch, seq, topk, tq, tk):
    nq = seq // tq
    n_qparts = 4
    pw = N_IDX_HEADS * IDX_DIM // n_qparts
    q_specs = [pl.BlockSpec((tq, pw), functools.partial(lambda b, q, part: (b * nq + q, COL_QI // pw + part), part=part))
               for part in range(n_qparts)]
    return pl.pallas_call(
        functools.partial(_idx_kernel, topk=topk, tk=tk, n_qparts=n_qparts),
        grid=(batch, nq),
        in_specs=q_specs + [
            pl.BlockSpec((tq, LANES), lambda b, q: (b * nq + q, 0)),
            pl.BlockSpec((None, IDX_DIM, seq), lambda b, q: (b, 0, 0)),
        ],
        out_specs=pl.BlockSpec((tq, seq), lambda b, q: (b * nq + q, 0)),
        out_shape=jax.ShapeDtypeStruct((batch * seq, seq), F32),
        scratch_shapes=[pltpu.VMEM((N_IDX_HEADS * tq, IDX_DIM), BF16),
                        pltpu.VMEM((N_IDX_HEADS, tq, LANES), F32),
                        pltpu.VMEM((tq, seq), jnp.int32)],
        compiler_params=_cparams(("parallel", "arbitrary")),
        name="prompt_indexer",
    )(*([zq] * n_qparts), wi, kit)


def _online_softmax_step(s, m, l):
    m_new = jnp.maximum(m, jnp.max(s, axis=1, keepdims=True))
    p = jnp.exp2(s - m_new)
    alpha = jnp.exp2(m - m_new)
    return p, alpha, m_new, alpha * l + jnp.sum(p, axis=1, keepdims=True)


def _dsa_kernel(q_ref, kt_ref, v_ref, bias_ref, o_ref, acc_scr, *, tk):
    qb = pl.program_id(2)
    tq = q_ref.shape[0]
    g_a = q_ref.shape[1] // HEAD_DIM
    nkb = (qb * tq + tq + tk - 1) // tk
    q = jnp.concatenate([q_ref[:, r * HEAD_DIM:(r + 1) * HEAD_DIM] for r in range(g_a)], axis=0)
    acc_scr[...] = jnp.zeros(acc_scr.shape, F32)

    def body(kb, carry):
        m, l = carry
        off = pl.multiple_of(kb * tk, tk)
        b = bias_ref[:, pl.ds(off, tk)]
        s = jnp.dot(q, kt_ref[:, pl.ds(off, tk)], preferred_element_type=F32) + jnp.concatenate([b] * g_a, axis=0)
        p, alpha, m, l = _online_softmax_step(s, m, l)
        pv = jnp.dot(p.astype(BF16), v_ref[pl.ds(off, tk), :], preferred_element_type=F32)
        acc_scr[...] = alpha * acc_scr[...] + pv
        return m, l

    init = (jnp.full((g_a * tq, 1), NEG, F32), jnp.zeros((g_a * tq, 1), F32))
    m, l = lax.fori_loop(0, nkb, body, init)
    out = acc_scr[...] / l
    for r in range(g_a):
        o_ref[:, r * HEAD_DIM:(r + 1) * HEAD_DIM] = out[r * tq:(r + 1) * tq].astype(o_ref.dtype)


def _dsa_prompt(zq, kat, bias, batch, seq, tq, tk):
    nq = seq // tq
    gw = (N_HEADS_A // N_KV_A) * HEAD_DIM
    return pl.pallas_call(
        functools.partial(_dsa_kernel, tk=tk),
        grid=(batch, N_KV_A, nq),
        in_specs=[
            pl.BlockSpec((tq, gw), lambda b, g, q: (b * nq + q, COL_QA // gw + g)),
            pl.BlockSpec((None, None, HEAD_DIM, seq), lambda b, g, q: (b, g, 0, 0)),
            pl.BlockSpec((seq, HEAD_DIM), lambda b, g, q: (b, COL_VA // HEAD_DIM + g)),
            pl.BlockSpec((tq, seq), lambda b, g, q: (b * nq + q, 0)),
        ],
        out_specs=pl.BlockSpec((tq, gw), lambda b, g, q: (b * nq + q, g)),
        out_shape=jax.ShapeDtypeStruct((batch * seq, MIX_A), BF16),
        scratch_shapes=[pltpu.VMEM(((N_HEADS_A // N_KV_A) * tq, HEAD_DIM), F32)],
        compiler_params=_cparams(("parallel", "parallel", "arbitrary")),
        name="dsa_prompt",
    )(zq, kat, zq, bias)


def _lambda(lq1, lk1, lq2, lk2, lam_init):
    return (jnp.exp(jnp.sum(lq1[...] * lk1[...], axis=1, keepdims=True))
            - jnp.exp(jnp.sum(lq2[...] * lk2[...], axis=1, keepdims=True)) + lam_init)


def _diff_kernel(q_ref, kt_ref, v_ref, lq1, lk1, lq2, lk2, gn_ref, o_ref, acc_scr, *, tk, lam_init):
    qb = pl.program_id(2)
    tq = q_ref.shape[0]
    g_b = q_ref.shape[1] // (2 * HEAD_DIM)
    rows_n = g_b * tq
    row0 = qb * tq
    nkb = (row0 + tq + tk - 1) // tk
    n_full = (row0 + 1) // tk
    qs = [jnp.concatenate([q_ref[:, (2 * r + c) * HEAD_DIM:(2 * r + c + 1) * HEAD_DIM] for r in range(g_b)], axis=0)
          for c in range(2)]
    acc_scr[...] = jnp.zeros(acc_scr.shape, F32)
    ri = lax.broadcasted_iota(jnp.int32, (tq, tk), 0)
    rows = row0 + jnp.concatenate([ri] * g_b, axis=0)
    cols = lax.broadcasted_iota(jnp.int32, (rows_n, tk), 1)

    def make_body(masked):
        def body(kb, carry):
            off = pl.multiple_of(kb * tk, tk)
            v = v_ref[pl.ds(off, tk), :]
            new = []
            for c in range(2):
                m, l = carry[c]
                s = jnp.dot(qs[c], kt_ref[c, :, pl.ds(off, tk)], preferred_element_type=F32)
                if masked:
                    s = jnp.where(off + cols <= rows, s, NEG)
                p, alpha, m, l = _online_softmax_step(s, m, l)
                acc_scr[c] = alpha * acc_scr[c] + jnp.dot(p.astype(BF16), v, preferred_element_type=F32)
                new.append((m, l))
            return tuple(new)
        return body

    one = (jnp.full((rows_n, 1), NEG, F32), jnp.zeros((rows_n, 1), F32))
    carry = lax.fori_loop(0, n_full, make_body(False), (one, one))
    carry = lax.fori_loop(n_full, nkb, make_body(True), carry)
    lam = _lambda(lq1, lk1, lq2, lk2, lam_init)
    o = acc_scr[0] / carry[0][1] - lam * (acc_scr[1] / carry[1][1])
    ms = jnp.mean(o * o, axis=-1, keepdims=True)
    o = o * lax.rsqrt(ms + EPS) * gn_ref[...] * (1.0 - lam_init)
    vw = 2 * HEAD_DIM
    for r in range(g_b):
        o_ref[:, r * vw:(r + 1) * vw] = o[r * tq:(r + 1) * tq].astype(o_ref.dtype)


def _diff_prompt(zq, kbt, lams, gn, batch, seq, tq, tk, lam_init):
    nq = seq // tq
    g_b = N_HEADS_B // N_KV_B
    gw = g_b * 2 * HEAD_DIM
    vw = 2 * HEAD_DIM
    vec = pl.BlockSpec((1, HEAD_DIM), lambda b, g, q: (0, 0))
    return pl.pallas_call(
        functools.partial(_diff_kernel, tk=tk, lam_init=lam_init),
        grid=(batch, N_KV_B, nq),
        in_specs=[
            pl.BlockSpec((tq, gw), lambda b, g, q: (b * nq + q, COL_QB // gw + g)),
            pl.BlockSpec((None, 2, HEAD_DIM, seq), lambda b, g, q: (b, g, 0, 0)),
            pl.BlockSpec((seq, vw), lambda b, g, q: (b, COL_VB // vw + g)),
            vec, vec, vec, vec,
            pl.BlockSpec((1, vw), lambda b, g, q: (0, 0)),
        ],
        out_specs=pl.BlockSpec((tq, gw), lambda b, g, q: (b * nq + q, g)),
        out_shape=jax.ShapeDtypeStruct((batch * seq, MIX_B), BF16),
        scratch_shapes=[pltpu.VMEM((2, g_b * tq, vw), F32)],
        compiler_params=_cparams(("parallel", "parallel", "arbitrary")),
        name="diff_prompt",
    )(zq, kbt, zq, *lams, gn)


def _out_proj_kernel(a_ref, b_ref, w_ref, x_ref, o_ref):
    ka = a_ref.shape[1]
    acc = jnp.dot(a_ref[...], w_ref[0:ka, :], preferred_element_type=F32)
    acc = acc + jnp.dot(b_ref[...], w_ref[ka:, :], preferred_element_type=F32)
    o_ref[...] = x_ref[...] + acc


def _out_proj(oa, ob, w, x, tm, tn):
    m, n = x.shape
    return pl.pallas_call(
        _out_proj_kernel,
        grid=(n // tn, m // tm),
        in_specs=[
            pl.BlockSpec((tm, oa.shape[1]), lambda j, i: (i, 0)),
            pl.BlockSpec((tm, ob.shape[1]), lambda j, i: (i, 0)),
            pl.BlockSpec((w.shape[0], tn), lambda j, i: (0, j)),
            pl.BlockSpec((tm, tn), lambda j, i: (i, j)),
        ],
        out_specs=pl.BlockSpec((tm, tn), lambda j, i: (i, j)),
        out_shape=jax.ShapeDtypeStruct((m, n), F32),
        compiler_params=_cparams(("parallel", "parallel")),
        name="out_proj",
    )(oa, ob, w, x)


def _mlp_up_kernel(x_ref, g_ref, w_ref, u_ref, h_scr):
    @pl.when(pl.program_id(1) == 0)
    def _():
        _rms_rows(x_ref, g_ref, h_scr, min(64, x_ref.shape[0]))

    a = jnp.maximum(jnp.dot(h_scr[...], w_ref[...], preferred_element_type=F32), 0.0)
    u_ref[...] = (a * a).astype(u_ref.dtype)


def _mlp_up(x, g, w, tm, tn):
    m, d = x.shape
    f = w.shape[1]
    return pl.pallas_call(
        _mlp_up_kernel,
        grid=(m // tm, f // tn),
        in_specs=[
            pl.BlockSpec((tm, d), lambda i, j: (i, 0)),
            pl.BlockSpec((1, d), lambda i, j: (0, 0)),
            pl.BlockSpec((d, tn), lambda i, j: (0, j)),
        ],
        out_specs=pl.BlockSpec((tm, tn), lambda i, j: (i, j)),
        out_shape=jax.ShapeDtypeStruct((m, f), BF16),
        scratch_shapes=[pltpu.VMEM((tm, d), BF16)],
        compiler_params=_cparams(("parallel", "arbitrary")),
        name="mlp_up",
    )(x, g, w)


def _mlp_down_kernel(u_ref, w_ref, x_ref, g_ref, o_ref, *, final_norm):
    k = pl.program_id(1)

    @pl.when(k == 0)
    def _():
        o_ref[...] = x_ref[...]

    o_ref[...] += jnp.dot(u_ref[...], w_ref[...], preferred_element_type=F32)

    if final_norm:
        @pl.when(k == pl.num_programs(1) - 1)
        def _():
            _rms_rows(o_ref, g_ref, o_ref, min(64, o_ref.shape[0]))


def _mlp_down(u, w, x, g, tm, tk, final_norm):
    m, d = x.shape
    f = u.shape[1]
    return pl.pallas_call(
        functools.partial(_mlp_down_kernel, final_norm=final_norm),
        grid=(m // tm, f // tk),
        in_specs=[
            pl.BlockSpec((tm, tk), lambda i, k: (i, k)),
            pl.BlockSpec((tk, d), lambda i, k: (k, 0)),
            pl.BlockSpec((tm, d), lambda i, k: (i, 0)),
            pl.BlockSpec((1, d), lambda i, k: (0, 0)),
        ],
        out_specs=pl.BlockSpec((tm, d), lambda i, k: (i, 0)),
        out_shape=jax.ShapeDtypeStruct((m, d), F32),
        compiler_params=_cparams(("parallel", "arbitrary")),
        name="mlp_down",
    )(u, w, x, g)


def _nt_dot(a, b):
    return lax.dot_general(a, b, (((1,), (1,)), ((), ())), preferred_element_type=F32)


def _sidx_kernel(pt_ref, q_ref, w_ref, knew_ref, *rest):
    n_pages = len(rest) - 2
    page_refs, sc_ref, self_ref = rest[:n_pages], rest[n_pages], rest[n_pages + 1]
    q = q_ref[...]
    w = w_ref[...] * (IDX_DIM ** -0.5)
    for r in range(n_pages):
        lg = _nt_dot(q, page_refs[r][...].astype(BF16))
        sc_ref[:, r * PAGE_SIZE:(r + 1) * PAGE_SIZE] = jnp.sum(jnp.maximum(lg, 0.0) * w, axis=0, keepdims=True)
    kn = knew_ref[...].astype(BF16).astype(F32)
    lg = jnp.sum(q.astype(F32) * kn, axis=1, keepdims=True)
    self_ref[...] = jnp.broadcast_to(jnp.sum(jnp.maximum(lg, 0.0) * w, axis=0, keepdims=True), self_ref.shape)


def _sample_scores(page_table, qi_s, wcol, ki_new, cache_k_idx_l):
    db, n_pages = page_table.shape
    page_specs = [pl.BlockSpec((None, PAGE_SIZE, IDX_DIM), functools.partial(lambda b, pt, r: (pt[b, r], 0, 0), r=r))
                  for r in range(n_pages)]
    return pl.pallas_call(
        _sidx_kernel,
        grid_spec=pltpu.PrefetchScalarGridSpec(
            num_scalar_prefetch=1,
            grid=(db,),
            in_specs=[
                pl.BlockSpec((None, N_IDX_HEADS, IDX_DIM), lambda b, pt: (b, 0, 0)),
                pl.BlockSpec((None, N_IDX_HEADS, 1), lambda b, pt: (b, 0, 0)),
                pl.BlockSpec((None, 1, IDX_DIM), lambda b, pt: (b, 0, 0)),
            ] + page_specs,
            out_specs=[
                pl.BlockSpec((None, 1, n_pages * PAGE_SIZE), lambda b, pt: (b, 0, 0)),
                pl.BlockSpec((None, 1, LANES), lambda b, pt: (b, 0, 0)),
            ],
        ),
        out_shape=[jax.ShapeDtypeStruct((db, 1, n_pages * PAGE_SIZE), F32),
                   jax.ShapeDtypeStruct((db, 1, LANES), F32)],
        compiler_params=_cparams(("parallel",)),
        name="sample_scores",
    )(page_table, qi_s, wcol, ki_new, *([cache_k_idx_l] * n_pages))


def _stopk_kernel(sc_ref, self_ref, bias_ref, bself_ref, *, topk):
    db, past = sc_ref.shape
    keys = _sort_key(sc_ref[...])
    kself = _sort_key(self_ref[...])

    def count_ge(cand):
        c = jnp.sum(jnp.where(keys >= cand, 1.0, 0.0), axis=1, keepdims=True)
        return c + jnp.where(kself[:, 0:1] >= cand, 1.0, 0.0)

    thr = _kth_largest(count_ge, topk, db)
    bias_ref[...] = jnp.where(keys >= thr, 0.0, NEG)
    bself_ref[...] = jnp.where(kself >= thr, 0.0, NEG)


def _sample_topk(scores, self_scores, topk):
    db, past = scores.shape
    return pl.pallas_call(
        functools.partial(_stopk_kernel, topk=topk),
        out_shape=[jax.ShapeDtypeStruct((db, past), F32), jax.ShapeDtypeStruct((db, LANES), F32)],
        compiler_params=pltpu.CompilerParams(vmem_limit_bytes=VMEM_LIMIT),
        name="sample_topk",
    )(scores, self_scores)


def _sattn_kernel(pt_ref, qa_ref, qb_ref, bias_ref, bself_ref, kas_ref, vas_ref, kbs_ref, vbs_ref,
                  lq1, lk1, lq2, lk2, gn_ref, *rest, pg, lam_init):
    ka_refs, va_refs, kb_refs, vb_refs = rest[0:pg], rest[pg:2 * pg], rest[2 * pg:3 * pg], rest[3 * pg:4 * pg]
    oa_ref, ob_ref, ma, la, acca, mb, lb, accb = rest[4 * pg:]
    step = pl.program_id(1)
    nh = N_HEADS_A

    @pl.when(step == 0)
    def _():
        ma[...] = jnp.full(ma.shape, NEG, F32)
        mb[...] = jnp.full(mb.shape, NEG, F32)
        la[...] = jnp.zeros(la.shape, F32)
        lb[...] = jnp.zeros(lb.shape, F32)
        acca[...] = jnp.zeros(acca.shape, F32)
        accb[...] = jnp.zeros(accb.shape, F32)

    qa = qa_ref[...]
    qb = qb_ref[...]
    row = lax.broadcasted_iota(jnp.int32, (nh, LANES), 0)
    grp_a = row // (N_HEADS_A // N_KV_A)
    grp_b = (row % N_HEADS_B) // (N_HEADS_B // N_KV_B)
    cmp_b = row // N_HEADS_B

    sa, sb = [], []
    for r in range(pg):
        s = jnp.zeros((nh, PAGE_SIZE), F32)
        for g in range(N_KV_A):
            kg = ka_refs[r][pl.ds(g, PAGE_SIZE, stride=N_KV_A), :].astype(BF16)
            s = jnp.where(grp_a == g, _nt_dot(qa, kg), s)
        sa.append(s + bias_ref[:, r * PAGE_SIZE:(r + 1) * PAGE_SIZE])
        s = jnp.zeros((nh, PAGE_SIZE), F32)
        for g in range(N_KV_B):
            for c in range(2):
                kg = kb_refs[r][pl.ds(g * 2 + c, PAGE_SIZE, stride=2 * N_KV_B), :].astype(BF16)
                s = jnp.where((grp_b == g) & (cmp_b == c), _nt_dot(qb, kg), s)
        sb.append(s)
    sa = jnp.concatenate(sa, axis=1)
    sb = jnp.concatenate(sb, axis=1)

    pa, alpha_a, m_new, l_new = _online_softmax_step(sa, ma[...], la[...])
    ma[...] = m_new
    la[...] = l_new
    pb, alpha_b, m_new, l_new = _online_softmax_step(sb, mb[...], lb[...])
    mb[...] = m_new
    lb[...] = l_new
    pa = pa.astype(BF16)
    pb = pb.astype(BF16)
    rowv = lax.broadcasted_iota(jnp.int32, (nh, 2 * HEAD_DIM), 0)
    grp_bv = (rowv % N_HEADS_B) // (N_HEADS_B // N_KV_B)
    oa = jnp.zeros((nh, HEAD_DIM), F32)
    ob = jnp.zeros((nh, 2 * HEAD_DIM), F32)
    for r in range(pg):
        par = pa[:, r * PAGE_SIZE:(r + 1) * PAGE_SIZE]
        pbr = pb[:, r * PAGE_SIZE:(r + 1) * PAGE_SIZE]
        ta = jnp.zeros((nh, HEAD_DIM), F32)
        tb = jnp.zeros((nh, 2 * HEAD_DIM), F32)
        for g in range(N_KV_A):
            vg = va_refs[r][pl.ds(g, PAGE_SIZE, stride=N_KV_A), :].astype(BF16)
            ta = jnp.where(grp_a == g, jnp.dot(par, vg, preferred_element_type=F32), ta)
        for g in range(N_KV_B):
            vg = jnp.concatenate([vb_refs[r][pl.ds(cb * N_KV_B + g, PAGE_SIZE, stride=2 * N_KV_B), :]
                                  for cb in range(2)], axis=1).astype(BF16)
            tb = jnp.where(grp_bv == g, jnp.dot(pbr, vg, preferred_element_type=F32), tb)
        oa = oa + ta
        ob = ob + tb
    acca[...] = alpha_a * acca[...] + oa
    accb[...] = alpha_b * accb[...] + ob

    @pl.when(step == pl.num_programs(1) - 1)
    def _():
        rnd = lambda ref: ref[...].astype(BF16).astype(F32)
        s_self = jnp.sum(qa.astype(F32) * rnd(kas_ref), axis=1, keepdims=True) + bself_ref[:, 0:1]
        m_new = jnp.maximum(ma[...], s_self)
        p = jnp.exp2(s_self - m_new)
        alpha = jnp.exp2(ma[...] - m_new)
        l = alpha * la[...] + p
        o = alpha * acca[...] + p.astype(BF16).astype(F32) * rnd(vas_ref)
        oa_ref[...] = (o / l).astype(oa_ref.dtype)

        s_self = jnp.sum(qb.astype(F32) * rnd(kbs_ref), axis=1, keepdims=True)
        m_new = jnp.maximum(mb[...], s_self)
        p = jnp.exp2(s_self - m_new)
        alpha = jnp.exp2(mb[...] - m_new)
        l = alpha * lb[...] + p
        o = (alpha * accb[...] + p.astype(BF16).astype(F32) * rnd(vbs_ref)) / l
        lam = _lambda(lq1, lk1, lq2, lk2, lam_init)
        o = o[0:N_HEADS_B] - lam * o[N_HEADS_B:]
        ms = jnp.mean(o * o, axis=-1, keepdims=True)
        ob_ref[...] = (o * lax.rsqrt(ms + EPS) * gn_ref[...] * (1.0 - lam_init)).astype(ob_ref.dtype)


def _sample_attention(page_table, qa_s, qb_s, bias, bself, kas, vas, kbs, vbs, lams, gn,
                      ka2d, va2d, kb2d, vb2d, pg, lam_init):
    db, n_pages = page_table.shape
    nh = N_HEADS_A
    vw = 2 * HEAD_DIM

    def page_spec(shape, r):
        nd = len(shape)
        return pl.BlockSpec(shape, functools.partial(
            lambda b, s, pt, r: (pt[b, s * pg + r],) + (0,) * (nd - 1), r=r))

    per_b = lambda shape: pl.BlockSpec((None,) + shape, lambda b, s, pt: (b,) + (0,) * len(shape))
    vec = pl.BlockSpec((1, HEAD_DIM), lambda b, s, pt: (0, 0))
    in_specs = [
        per_b((nh, HEAD_DIM)), per_b((nh, HEAD_DIM)),
        pl.BlockSpec((None, 1, pg * PAGE_SIZE), lambda b, s, pt: (b, 0, s)),
        per_b((1, LANES)),
        per_b((nh, HEAD_DIM)), per_b((nh, HEAD_DIM)), per_b((nh, HEAD_DIM)), per_b((nh, vw)),
        vec, vec, vec, vec,
        pl.BlockSpec((1, vw), lambda b, s, pt: (0, 0)),
    ]
    in_specs += [page_spec((PAGE_SIZE * N_KV_A, HEAD_DIM), r) for r in range(pg)]
    in_specs += [page_spec((PAGE_SIZE * N_KV_A, HEAD_DIM), r) for r in range(pg)]
    in_specs += [page_spec((PAGE_SIZE * N_KV_B * 2, HEAD_DIM), r) for r in range(pg)]
    in_specs += [page_spec((PAGE_SIZE * N_KV_B * 2, HEAD_DIM), r) for r in range(pg)]
    return pl.pallas_call(
        functools.partial(_sattn_kernel, pg=pg, lam_init=lam_init),
        grid_spec=pltpu.PrefetchScalarGridSpec(
            num_scalar_prefetch=1,
            grid=(db, n_pages // pg),
            in_specs=in_specs,
            out_specs=[per_b((nh, HEAD_DIM)), per_b((N_HEADS_B, vw))],
            scratch_shapes=[
                pltpu.VMEM((nh, 1), F32), pltpu.VMEM((nh, 1), F32), pltpu.VMEM((nh, HEAD_DIM), F32),
                pltpu.VMEM((nh, 1), F32), pltpu.VMEM((nh, 1), F32), pltpu.VMEM((nh, vw), F32),
            ],
        ),
        out_shape=[jax.ShapeDtypeStruct((db, nh, HEAD_DIM), BF16),
                   jax.ShapeDtypeStruct((db, N_HEADS_B, vw), BF16)],
        compiler_params=_cparams(("parallel", "arbitrary")),
        name="sample_attention",
    )(page_table, qa_s, qb_s, bias, bself, kas, vas, kbs, vbs, *lams, gn,
      *([ka2d] * pg), *([va2d] * pg), *([kb2d] * pg), *([vb2d] * pg))


def _rope_tables(pos):
    half = ROT_DIM // 2
    inv = jnp.power(ROPE_THETA, -jnp.arange(half, dtype=F32) / half)
    ang = pos.astype(F32)[:, None] * inv[None, :]
    cos, sin = jnp.cos(ang), jnp.sin(ang)
    n = pos.shape[0]
    z = lambda w: jnp.zeros((n, w), F32)
    cos_t = jnp.concatenate([cos, cos, jnp.ones((n, HEAD_DIM - ROT_DIM), F32)], axis=1)
    s1_t = jnp.concatenate([-sin, z(HEAD_DIM - half)], axis=1)
    s2_t = jnp.concatenate([z(half), sin, z(HEAD_DIM - ROT_DIM)], axis=1)
    ident = (jnp.ones((n, LANES), F32), z(LANES), z(LANES))
    return tuple(jnp.stack([t, e, t * Q_SCALE]) for t, e in zip((cos_t, s1_t, s2_t), ident))


def _tile(n, pref):
    return pref if n % pref == 0 else n


def _split_kv32(kv32):
    w = PROJ_TN
    return kv32[:, 0:w], kv32[:, w:2 * w], kv32[:, 2 * w:4 * w], kv32[:, 4 * w:6 * w]


def kernel(x_prompt, x_sample, cache_k_a, cache_v_a, cache_k_b, cache_v_b, cache_k_idx, page_table, norm1, w_in, k_idx_norm_w, k_idx_norm_b, lambda_q1, lambda_k1, lambda_q2, lambda_k2, diff_norm_w, w_out, norm2, w_up, w_down, norm_f):
    batch, seq, d_model = x_prompt.shape
    db, dec_seq, _ = x_sample.shape
    assert dec_seq == 1
    depth = w_in.shape[0]
    n_pages = page_table.shape[1]
    past = n_pages * PAGE_SIZE
    n_pool = cache_k_a.shape[1]
    topk_p = min(TOPK_MAX, seq // 4)
    topk_s = min(TOPK_MAX, (past + dec_seq) // 4)

    mp = batch * seq
    tm_p = _tile(mp, 512)
    tq_a = _tile(seq, 128)
    tq_b = _tile(seq, 256)
    tq_i = _tile(seq, 256)
    tk_att = _tile(seq, 512)
    tk_idx = _tile(seq, 256)
    pg = 8 if n_pages % 8 == 0 else n_pages

    tab_p = _rope_tables(jnp.arange(seq))
    tab_s = _rope_tables(jnp.full((db,), past, jnp.int32))
    row = lambda a: a.reshape(1, -1)

    xp = x_prompt.reshape(mp, d_model)
    xs = x_sample.reshape(db, d_model)
    st = [[] for _ in range(10)]
    for l in range(depth):
        lam_init = 0.8 - 0.6 * math.exp(-0.3 * l)
        w_in_b = w_in[l, :, :COL_TAIL].astype(BF16)
        w_tail = jnp.pad(w_in[l, :, COL_TAIL:], ((0, 0), (0, TAIL_WIDTH - (w_in.shape[2] - COL_TAIL)))).astype(BF16)
        w_out_b = w_out[l].astype(BF16)
        w_up_b = w_up[l].astype(BF16)
        w_down_b = w_down[l].astype(BF16)
        lams = (row(lambda_q1[l]), row(lambda_k1[l]), row(lambda_q2[l]), row(lambda_k2[l]))
        gn = row(diff_norm_w[l])
        n1, n2 = row(norm1[l]), row(norm2[l])
        lnw, lnb = row(k_idx_norm_w[l]), row(k_idx_norm_b[l])
        last = l == depth - 1

        zq, kv32 = _project(xp, n1, w_in_b, tab_p, tm_p, seq // tm_p)
        ki, kibf, wi = _project_tail(xp, n1, w_tail, tab_p, lnw, lnb, tm_p, seq // tm_p)
        ka, va, kb, vb = _split_kv32(kv32)
        kit = kibf.reshape(batch, seq, IDX_DIM).transpose(0, 2, 1)
        kat = zq[:, COL_KA:COL_VA].reshape(batch, seq, N_KV_A, HEAD_DIM).transpose(0, 2, 3, 1)
        kbt = zq[:, COL_KB:COL_VB].reshape(batch, seq, N_KV_B * 2, HEAD_DIM).transpose(0, 2, 3, 1)
        bias = _prompt_indexer(zq, wi, kit, batch, seq, topk_p, tq_i, tk_idx)
        oa = _dsa_prompt(zq, kat, bias, batch, seq, tq_a, tk_att)
        ob = _diff_prompt(zq, kbt, lams, gn, batch, seq, tq_b, tk_att, lam_init)
        xp = _out_proj(oa, ob, w_out_b, xp, tm_p, 1024)
        u = _mlp_up(xp, n2, w_up_b, tm_p, 1024)
        xp = _mlp_down(u, w_down_b, xp, row(norm_f), tm_p, 512, last)
        st[0].append(ka.reshape(batch, seq, N_KV_A, HEAD_DIM))
        st[1].append(va.reshape(batch, seq, N_KV_A, HEAD_DIM))
        st[2].append(kb.reshape(batch, seq, N_KV_B, 2, HEAD_DIM))
        st[3].append(vb.reshape(batch, seq, N_KV_B, 2 * HEAD_DIM))
        st[4].append(ki.reshape(batch, seq, IDX_DIM))

        zq_s, kv32_s = _project(xs, n1, w_in_b, tab_s, db, 1)
        ki_s, _, wi_s = _project_tail(xs, n1, w_tail, tab_s, lnw, lnb, db, 1)
        ka_s, va_s, kb_s, vb_s = _split_kv32(kv32_s)
        qi_s = zq_s[:, COL_QI:COL_TAIL].reshape(db, N_IDX_HEADS, IDX_DIM)
        wcol = wi_s[:, :N_IDX_HEADS].reshape(db, N_IDX_HEADS, 1)
        scores, self_sc = _sample_scores(page_table, qi_s, wcol, ki_s.reshape(db, 1, IDX_DIM), cache_k_idx[l])
        bias_s, bself = _sample_topk(scores.reshape(db, past), self_sc.reshape(db, LANES), topk_s)
        qa_s = zq_s[:, COL_QA:COL_KA].reshape(db, N_HEADS_A, HEAD_DIM)
        qb_s = zq_s[:, COL_QB:COL_KB].reshape(db, N_HEADS_B, 2, HEAD_DIM).transpose(0, 2, 1, 3).reshape(
            db, 2 * N_HEADS_B, HEAD_DIM)
        rows_a = jnp.arange(N_HEADS_A) // (N_HEADS_A // N_KV_A)
        rr = jnp.arange(2 * N_HEADS_B)
        grp_b = (rr % N_HEADS_B) // (N_HEADS_B // N_KV_B)
        rows_kb = grp_b * 2 + rr // N_HEADS_B
        kas = ka_s.reshape(db, N_KV_A, HEAD_DIM)[:, rows_a]
        vas = va_s.reshape(db, N_KV_A, HEAD_DIM)[:, rows_a]
        kbs = kb_s.reshape(db, N_KV_B * 2, HEAD_DIM)[:, rows_kb]
        vbs = vb_s.reshape(db, N_KV_B, 2 * HEAD_DIM)[:, grp_b]
        rows_per_page = PAGE_SIZE * N_KV_B * 2
        vb2d = cache_v_b[l].reshape(n_pool, PAGE_SIZE, N_KV_B, 2, HEAD_DIM).transpose(0, 1, 3, 2, 4).reshape(
            n_pool * rows_per_page, HEAD_DIM)
        oa_s, ob_s = _sample_attention(
            page_table, qa_s, qb_s, bias_s.reshape(db, 1, past), bself.reshape(db, 1, LANES), kas, vas, kbs, vbs,
            lams, gn,
            cache_k_a[l].reshape(n_pool * PAGE_SIZE * N_KV_A, HEAD_DIM),
            cache_v_a[l].reshape(n_pool * PAGE_SIZE * N_KV_A, HEAD_DIM),
            cache_k_b[l].reshape(n_pool * rows_per_page, HEAD_DIM),
            vb2d, pg, lam_init)
        xs = _out_proj(oa_s.reshape(db, MIX_A), ob_s.reshape(db, MIX_B), w_out_b, xs, db, 1024)
        u_s = _mlp_up(xs, n2, w_up_b, db, 1024)
        xs = _mlp_down(u_s, w_down_b, xs, row(norm_f), db, 512, last)
        st[5].append(ka_s.reshape(db, dec_seq, N_KV_A, HEAD_DIM))
        st[6].append(va_s.reshape(db, dec_seq, N_KV_A, HEAD_DIM))
        st[7].append(kb_s.reshape(db, dec_seq, N_KV_B, 2, HEAD_DIM))
        st[8].append(vb_s.reshape(db, dec_seq, N_KV_B, 2 * HEAD_DIM))
        st[9].append(ki_s.reshape(db, dec_seq, IDX_DIM))
    outs = [jnp.stack(s, axis=0) for s in st]
    return (xp.reshape(batch, seq, d_model), xs.reshape(db, dec_seq, d_model), *outs)
```

```python
import functools
import math

import jax
import jax.numpy as jnp
from jax import lax
from jax.experimental import pallas as pl
from jax.experimental.pallas import tpu as pltpu

F32 = jnp.float32
BF16 = jnp.bfloat16

HEAD_DIM = 128
N_HEADS_A = 16
N_KV_A = 4
N_HEADS_B = 8
N_KV_B = 4
N_IDX_HEADS = 32
IDX_DIM = 128
TOPK_MAX = 256
ROPE_THETA = 500000.0
ROT_DIM = HEAD_DIM // 4
PAGE_SIZE = 128
EPS = 1e-6
MIX_A = N_HEADS_A * HEAD_DIM
MIX_B = N_HEADS_B * 2 * HEAD_DIM

LANES = 128
VMEM_LIMIT = 56 * 1024 * 1024
NEG = -1e30
Q_SCALE = HEAD_DIM ** -0.5 * math.log2(math.e)
INT_MIN = -2 ** 31
COUNT_ROWS = 64

PROJ_TN = 512
COL_QA, COL_KA, COL_VA, COL_QB, COL_KB, COL_VB, COL_QI, COL_TAIL = (
    0, 2048, 2560, 3072, 5120, 6144, 7168, 11264)
N_PROJ_TILES = COL_TAIL // PROJ_TN
TAIL_WIDTH = 2 * LANES
J_KA, J_VA, J_KB, J_VB, J_QI = (COL_KA // PROJ_TN, COL_VA // PROJ_TN, COL_KB // PROJ_TN,
                                COL_VB // PROJ_TN, COL_QI // PROJ_TN)
KV32_TILES = 7


def _cparams(sem):
    return pltpu.CompilerParams(dimension_semantics=sem, vmem_limit_bytes=VMEM_LIMIT)


def _rms_rows(x_ref, g_ref, out_ref, chunk):
    n = x_ref.shape[0] // chunk

    def body(c, carry):
        r = pl.multiple_of(c * chunk, chunk)
        x = x_ref[pl.ds(r, chunk), :].astype(F32)
        ms = jnp.mean(x * x, axis=-1, keepdims=True)
        out_ref[pl.ds(r, chunk), :] = (x * lax.rsqrt(ms + EPS) * g_ref[...]).astype(out_ref.dtype)
        return carry

    lax.fori_loop(0, n, body, 0)


def _rope(z, cos, s1, s2):
    return z * cos + pltpu.roll(z, LANES - ROT_DIM // 2, 1) * s1 + pltpu.roll(z, ROT_DIM // 2, 1) * s2


def _proj_kernel(x_ref, n1_ref, w_ref, cos_ref, s1_ref, s2_ref, zq_ref, kv_ref, h_scr):
    @pl.when(pl.program_id(1) == 0)
    def _():
        _rms_rows(x_ref, n1_ref, h_scr, min(64, x_ref.shape[0]))

    acc = jnp.dot(h_scr[...], w_ref[...], preferred_element_type=F32)
    cos, s1, s2 = cos_ref[...], s1_ref[...], s2_ref[...]
    for c in range(PROJ_TN // LANES):
        y = _rope(acc[:, c * LANES:(c + 1) * LANES], cos, s1, s2)
        zq_ref[:, c * LANES:(c + 1) * LANES] = y.astype(BF16)
        kv_ref[:, c * LANES:(c + 1) * LANES] = y


def _kv32_tile(j):
    kv_tiles = (J_KA, J_VA, J_KB, J_KB + 1, J_VB, J_VB + 1)
    return sum((j > t).astype(jnp.int32) for t in kv_tiles)


def _project(x, norm1, w_b, tables, tm, n_pos_blocks):
    m, d = x.shape
    tn = PROJ_TN

    def kind(j):
        is_value = (j == J_VA) | (j == J_VB) | (j == J_VB + 1)
        is_query = (j < J_KA) | ((j > J_VA) & (j < J_KB))
        return jnp.where(is_value, 1, jnp.where(is_query, 2, 0))

    tab = pl.BlockSpec((None, tm, LANES), lambda i, j: (kind(j), i % n_pos_blocks, 0))
    return pl.pallas_call(
        _proj_kernel,
        grid=(m // tm, N_PROJ_TILES),
        in_specs=[
            pl.BlockSpec((tm, d), lambda i, j: (i, 0)),
            pl.BlockSpec((1, d), lambda i, j: (0, 0)),
            pl.BlockSpec((d, tn), lambda i, j: (0, j)),
            tab, tab, tab,
        ],
        out_specs=[
            pl.BlockSpec((tm, tn), lambda i, j: (i, j)),
            pl.BlockSpec((tm, tn), lambda i, j: (i, _kv32_tile(j))),
        ],
        out_shape=[
            jax.ShapeDtypeStruct((m, COL_TAIL), BF16),
            jax.ShapeDtypeStruct((m, KV32_TILES * tn), F32),
        ],
        scratch_shapes=[pltpu.VMEM((tm, d), BF16)],
        compiler_params=_cparams(("parallel", "arbitrary")),
        name="in_proj",
    )(x, norm1, w_b, *tables)


def _tail_kernel(x_ref, n1_ref, w_ref, cos_ref, s1_ref, s2_ref, lnw_ref, lnb_ref, ki_ref, kibf_ref, wi_ref, h_scr):
    _rms_rows(x_ref, n1_ref, h_scr, min(64, x_ref.shape[0]))
    acc = jnp.dot(h_scr[...], w_ref[...], preferred_element_type=F32)
    z = acc[:, 0:IDX_DIM]
    mu = jnp.mean(z, axis=-1, keepdims=True)
    zc = z - mu
    var = jnp.mean(zc * zc, axis=-1, keepdims=True)
    y = zc * lax.rsqrt(var + EPS) * lnw_ref[...] + lnb_ref[...]
    y = _rope(y, cos_ref[...], s1_ref[...], s2_ref[...])
    ki_ref[...] = y
    kibf_ref[...] = y.astype(BF16)
    wi_ref[...] = acc[:, IDX_DIM:2 * IDX_DIM] * (N_IDX_HEADS ** -0.5)


def _project_tail(x, norm1, w_tail, tables, lnw, lnb, tm, n_pos_blocks):
    m, d = x.shape
    row = pl.BlockSpec((tm, LANES), lambda i: (i, 0))
    tab = pl.BlockSpec((None, tm, LANES), lambda i: (0, i % n_pos_blocks, 0))
    vec = pl.BlockSpec((1, LANES), lambda i: (0, 0))
    return pl.pallas_call(
        _tail_kernel,
        grid=(m // tm,),
        in_specs=[
            pl.BlockSpec((tm, d), lambda i: (i, 0)),
            pl.BlockSpec((1, d), lambda i: (0, 0)),
            pl.BlockSpec((d, TAIL_WIDTH), lambda i: (0, 0)),
            tab, tab, tab, vec, vec,
        ],
        out_specs=[row, row, row],
        out_shape=[jax.ShapeDtypeStruct((m, IDX_DIM), F32), jax.ShapeDtypeStruct((m, IDX_DIM), BF16),
                   jax.ShapeDtypeStruct((m, LANES), F32)],
        scratch_shapes=[pltpu.VMEM((tm, d), BF16)],
        compiler_params=_cparams(("parallel",)),
        name="idx_tail",
    )(x, norm1, w_tail, *tables, lnw, lnb)


def _sort_key(score):
    b = lax.bitcast_convert_type(score, jnp.int32)
    return jnp.where(b < 0, b ^ jnp.int32(0x7FFFFFFF), b)


def _kth_largest(count_ge, k, rows):
    kf = jnp.float32(k)
    c0 = count_ge(jnp.zeros((rows, 1), jnp.int32))
    prefix = jnp.where(c0 >= kf, jnp.int32(0), jnp.int32(INT_MIN))

    def body(i, prefix):
        cand = prefix | lax.shift_left(jnp.int32(1), jnp.int32(30) - i)
        return jnp.where(count_ge(cand) >= kf, cand, prefix)

    return lax.fori_loop(0, 31, body, prefix)


def _idx_kernel(*refs, topk, tk, n_qparts):
    q_refs = refs[:n_qparts]
    wi_ref, kt_ref, bias_ref, q_scr, wb_scr, key_scr = refs[n_qparts:]
    qb = pl.program_id(1)
    tq, seq = bias_ref.shape
    nh = N_IDX_HEADS
    heads_per_part = nh // n_qparts
    row0 = qb * tq
    nkb = (row0 + tq + tk - 1) // tk
    lanes_per_kb = tk // LANES

    w = wi_ref[...] * (IDX_DIM ** -0.5)
    for h in range(nh):
        wb_scr[h] = jnp.broadcast_to(w[:, h:h + 1], (tq, LANES))
        part, hh = divmod(h, heads_per_part)
        q_scr[h * tq:(h + 1) * tq, :] = q_refs[part][:, hh * IDX_DIM:(hh + 1) * IDX_DIM]

    rows = row0 + lax.broadcasted_iota(jnp.int32, (tq, LANES), 0)
    lane = lax.broadcasted_iota(jnp.int32, (tq, LANES), 1)

    def score_block(kb, carry):
        off = pl.multiple_of(kb * tk, tk)
        lg = jnp.dot(q_scr[...], kt_ref[:, pl.ds(off, tk)], preferred_element_type=F32)
        for c in range(lanes_per_kb):
            acc = jnp.zeros((tq, LANES), F32)
            for h in range(nh):
                acc = acc + jnp.maximum(lg[h * tq:(h + 1) * tq, c * LANES:(c + 1) * LANES], 0.0) * wb_scr[h]
            cols = off + c * LANES + lane
            key_scr[:, pl.ds(off + c * LANES, LANES)] = jnp.where(cols <= rows, _sort_key(acc), jnp.int32(INT_MIN))
        return carry

    lax.fori_loop(0, nkb, score_block, 0)

    @pl.when(nkb % 2 == 1)
    def _():
        key_scr[:, pl.ds(pl.multiple_of(nkb * tk, tk), tk)] = jnp.full((tq, tk), INT_MIN, jnp.int32)

    n_pairs = (nkb + 1) // 2
    rg = min(tq, COUNT_ROWS)

    def count_ge(cand):
        counts = []
        for r0 in range(0, tq, rg):
            candb = jnp.broadcast_to(cand[r0:r0 + rg], (rg, LANES))

            def body(kp, acc):
                off = pl.multiple_of(kp * 2 * tk, 2 * tk)
                for c in range(2 * lanes_per_kb):
                    kk = key_scr[r0:r0 + rg, pl.ds(off + c * LANES, LANES)]
                    acc = acc + jnp.where(kk >= candb, 1.0, 0.0)
                return acc

            acc = lax.fori_loop(0, n_pairs, body, jnp.zeros((rg, LANES), F32))
            counts.append(jnp.sum(acc, axis=1, keepdims=True))
        return jnp.concatenate(counts, axis=0)

    thr = jnp.broadcast_to(_kth_largest(count_ge, topk, tq), (tq, LANES))

    def write_bias(kb, carry):
        off = pl.multiple_of(kb * tk, tk)
        for c in range(lanes_per_kb):
            kk = key_scr[:, pl.ds(off + c * LANES, LANES)]
            sel = (kk >= thr) & ((off + c * LANES + lane) <= rows)
            bias_ref[:, pl.ds(off + c * LANES, LANES)] = jnp.where(sel, 0.0, NEG)
        return carry

    lax.fori_loop(0, nkb, write_bias, 0)

    def fill_neg(kb, carry):
        bias_ref[:, pl.ds(pl.multiple_of(kb * tk, tk), tk)] = jnp.full((tq, tk), NEG, F32)
        return carry

    lax.fori_loop(nkb, seq // tk, fill_neg, 0)


def _prompt_indexer(zq, wi, kit, batch, seq, topk, tq, tk):
    nq = seq // tq
    n_qparts = 4
    pw = N_IDX_HEADS * IDX_DIM // n_qparts
    q_specs = [pl.BlockSpec((tq, pw), functools.partial(lambda b, q, part: (b * nq + q, COL_QI // pw + part), part=part))
               for part in range(n_qparts)]
    return pl.pallas_call(
        functools.partial(_idx_kernel, topk=topk, tk=tk, n_qparts=n_qparts),
        grid=(batch, nq),
        in_specs=q_specs + [
            pl.BlockSpec((tq, LANES), lambda b, q: (b * nq + q, 0)),
            pl.BlockSpec((None, IDX_DIM, seq), lambda b, q: (b, 0, 0)),
        ],
        out_specs=pl.BlockSpec((tq, seq), lambda b, q: (b * nq + q, 0)),
        out_shape=jax.ShapeDtypeStruct((batch * seq, seq), F32),
        scratch_shapes=[pltpu.VMEM((N_IDX_HEADS * tq, IDX_DIM), BF16),
                        pltpu.VMEM((N_IDX_HEADS, tq, LANES), F32),
                        pltpu.VMEM((tq, pl.cdiv(seq, 2 * tk) * 2 * tk), jnp.int32)],
        compiler_params=_cparams(("parallel", "arbitrary")),
        name="prompt_indexer",
    )(*([zq] * n_qparts), wi, kit)


def _online_softmax_step(s, m, l):
    m_new = jnp.maximum(m, jnp.max(s, axis=1, keepdims=True))
    p = jnp.exp2(s - m_new)
    alpha = jnp.exp2(m - m_new)
    return p, alpha, m_new, alpha * l + jnp.sum(p, axis=1, keepdims=True)


def _dsa_kernel(q_ref, kt_ref, v_ref, bias_ref, o_ref, acc_scr, *, tk):
    qb = pl.program_id(2)
    tq = q_ref.shape[0]
    g_a = q_ref.shape[1] // HEAD_DIM
    nkb = (qb * tq + tq + tk - 1) // tk
    q = jnp.concatenate([q_ref[:, r * HEAD_DIM:(r + 1) * HEAD_DIM] for r in range(g_a)], axis=0)
    acc_scr[...] = jnp.zeros(acc_scr.shape, F32)

    def body(kb, carry):
        m, l = carry
        off = pl.multiple_of(kb * tk, tk)
        b = bias_ref[:, pl.ds(off, tk)]
        s = jnp.dot(q, kt_ref[:, pl.ds(off, tk)], preferred_element_type=F32) + jnp.concatenate([b] * g_a, axis=0)
        p, alpha, m, l = _online_softmax_step(s, m, l)
        pv = jnp.dot(p.astype(BF16), v_ref[pl.ds(off, tk), :], preferred_element_type=F32)
        acc_scr[...] = alpha * acc_scr[...] + pv
        return m, l

    init = (jnp.full((g_a * tq, 1), NEG, F32), jnp.zeros((g_a * tq, 1), F32))
    m, l = lax.fori_loop(0, nkb, body, init)
    out = acc_scr[...] / l
    for r in range(g_a):
        o_ref[:, r * HEAD_DIM:(r + 1) * HEAD_DIM] = out[r * tq:(r + 1) * tq].astype(o_ref.dtype)


def _dsa_prompt(zq, kat, bias, batch, seq, tq, tk):
    nq = seq // tq
    gw = (N_HEADS_A // N_KV_A) * HEAD_DIM
    return pl.pallas_call(
        functools.partial(_dsa_kernel, tk=tk),
        grid=(batch, N_KV_A, nq),
        in_specs=[
            pl.BlockSpec((tq, gw), lambda b, g, q: (b * nq + q, COL_QA // gw + g)),
            pl.BlockSpec((None, None, HEAD_DIM, seq), lambda b, g, q: (b, g, 0, 0)),
            pl.BlockSpec((seq, HEAD_DIM), lambda b, g, q: (b, COL_VA // HEAD_DIM + g)),
            pl.BlockSpec((tq, seq), lambda b, g, q: (b * nq + q, 0)),
        ],
        out_specs=pl.BlockSpec((tq, gw), lambda b, g, q: (b * nq + q, g)),
        out_shape=jax.ShapeDtypeStruct((batch * seq, MIX_A), BF16),
        scratch_shapes=[pltpu.VMEM(((N_HEADS_A // N_KV_A) * tq, HEAD_DIM), F32)],
        compiler_params=_cparams(("parallel", "parallel", "arbitrary")),
        name="dsa_prompt",
    )(zq, kat, zq, bias)


def _lambda(lq1, lk1, lq2, lk2, lam_init):
    return (jnp.exp(jnp.sum(lq1[...] * lk1[...], axis=1, keepdims=True))
            - jnp.exp(jnp.sum(lq2[...] * lk2[...], axis=1, keepdims=True)) + lam_init)


def _diff_kernel(q_ref, kt_ref, v_ref, lq1, lk1, lq2, lk2, gn_ref, o_ref, acc_scr, *, tk, lam_init):
    qb = pl.program_id(2)
    tq = q_ref.shape[0]
    g_b = q_ref.shape[1] // (2 * HEAD_DIM)
    rows_n = g_b * tq
    row0 = qb * tq
    nkb = (row0 + tq + tk - 1) // tk
    n_full = (row0 + 1) // tk
    qs = [jnp.concatenate([q_ref[:, (2 * r + c) * HEAD_DIM:(2 * r + c + 1) * HEAD_DIM] for r in range(g_b)], axis=0)
          for c in range(2)]
    acc_scr[...] = jnp.zeros(acc_scr.shape, F32)
    ri = lax.broadcasted_iota(jnp.int32, (tq, tk), 0)
    rows = row0 + jnp.concatenate([ri] * g_b, axis=0)
    cols = lax.broadcasted_iota(jnp.int32, (rows_n, tk), 1)

    def make_body(masked):
        def body(kb, carry):
            off = pl.multiple_of(kb * tk, tk)
            v = v_ref[pl.ds(off, tk), :]
            new = []
            for c in range(2):
                m, l = carry[c]
                s = jnp.dot(qs[c], kt_ref[c, :, pl.ds(off, tk)], preferred_element_type=F32)
                if masked:
                    s = jnp.where(off + cols <= rows, s, NEG)
                p, alpha, m, l = _online_softmax_step(s, m, l)
                acc_scr[c] = alpha * acc_scr[c] + jnp.dot(p.astype(BF16), v, preferred_element_type=F32)
                new.append((m, l))
            return tuple(new)
        return body

    one = (jnp.full((rows_n, 1), NEG, F32), jnp.zeros((rows_n, 1), F32))
    carry = lax.fori_loop(0, n_full, make_body(False), (one, one))
    carry = lax.fori_loop(n_full, nkb, make_body(True), carry)
    lam = _lambda(lq1, lk1, lq2, lk2, lam_init)
    o = acc_scr[0] / carry[0][1] - lam * (acc_scr[1] / carry[1][1])
    ms = jnp.mean(o * o, axis=-1, keepdims=True)
    o = o * lax.rsqrt(ms + EPS) * gn_ref[...] * (1.0 - lam_init)
    vw = 2 * HEAD_DIM
    for r in range(g_b):
        o_ref[:, r * vw:(r + 1) * vw] = o[r * tq:(r + 1) * tq].astype(o_ref.dtype)


def _diff_prompt(zq, kbt, lams, gn, batch, seq, tq, tk, lam_init):
    nq = seq // tq
    g_b = N_HEADS_B // N_KV_B
    gw = g_b * 2 * HEAD_DIM
    vw = 2 * HEAD_DIM
    vec = pl.BlockSpec((1, HEAD_DIM), lambda b, g, q: (0, 0))
    return pl.pallas_call(
        functools.partial(_diff_kernel, tk=tk, lam_init=lam_init),
        grid=(batch, N_KV_B, nq),
        in_specs=[
            pl.BlockSpec((tq, gw), lambda b, g, q: (b * nq + q, COL_QB // gw + g)),
            pl.BlockSpec((None, 2, HEAD_DIM, seq), lambda b, g, q: (b, g, 0, 0)),
            pl.BlockSpec((seq, vw), lambda b, g, q: (b, COL_VB // vw + g)),
            vec, vec, vec, vec,
            pl.BlockSpec((1, vw), lambda b, g, q: (0, 0)),
        ],
        out_specs=pl.BlockSpec((tq, gw), lambda b, g, q: (b * nq + q, g)),
        out_shape=jax.ShapeDtypeStruct((batch * seq, MIX_B), BF16),
        scratch_shapes=[pltpu.VMEM((2, g_b * tq, vw), F32)],
        compiler_params=_cparams(("parallel", "parallel", "arbitrary")),
        name="diff_prompt",
    )(zq, kbt, zq, *lams, gn)


def _out_proj_kernel(a_ref, b_ref, w_ref, x_ref, o_ref):
    ka = a_ref.shape[1]
    acc = jnp.dot(a_ref[...], w_ref[0:ka, :], preferred_element_type=F32)
    acc = acc + jnp.dot(b_ref[...], w_ref[ka:, :], preferred_element_type=F32)
    o_ref[...] = x_ref[...] + acc


def _out_proj(oa, ob, w, x, tm, tn):
    m, n = x.shape
    return pl.pallas_call(
        _out_proj_kernel,
        grid=(n // tn, m // tm),
        in_specs=[
            pl.BlockSpec((tm, oa.shape[1]), lambda j, i: (i, 0)),
            pl.BlockSpec((tm, ob.shape[1]), lambda j, i: (i, 0)),
            pl.BlockSpec((w.shape[0], tn), lambda j, i: (0, j)),
            pl.BlockSpec((tm, tn), lambda j, i: (i, j)),
        ],
        out_specs=pl.BlockSpec((tm, tn), lambda j, i: (i, j)),
        out_shape=jax.ShapeDtypeStruct((m, n), F32),
        compiler_params=_cparams(("parallel", "parallel")),
        name="out_proj",
    )(oa, ob, w, x)


def _mlp_up_kernel(x_ref, g_ref, w_ref, u_ref, h_scr):
    @pl.when(pl.program_id(1) == 0)
    def _():
        _rms_rows(x_ref, g_ref, h_scr, min(64, x_ref.shape[0]))

    a = jnp.maximum(jnp.dot(h_scr[...], w_ref[...], preferred_element_type=F32), 0.0)
    u_ref[...] = (a * a).astype(u_ref.dtype)


def _mlp_up(x, g, w, tm, tn):
    m, d = x.shape
    f = w.shape[1]
    return pl.pallas_call(
        _mlp_up_kernel,
        grid=(m // tm, f // tn),
        in_specs=[
            pl.BlockSpec((tm, d), lambda i, j: (i, 0)),
            pl.BlockSpec((1, d), lambda i, j: (0, 0)),
            pl.BlockSpec((d, tn), lambda i, j: (0, j)),
        ],
        out_specs=pl.BlockSpec((tm, tn), lambda i, j: (i, j)),
        out_shape=jax.ShapeDtypeStruct((m, f), BF16),
        scratch_shapes=[pltpu.VMEM((tm, d), BF16)],
        compiler_params=_cparams(("parallel", "arbitrary")),
        name="mlp_up",
    )(x, g, w)


def _mlp_down_kernel(u_ref, w_ref, x_ref, g_ref, o_ref, *, final_norm):
    k = pl.program_id(1)

    @pl.when(k == 0)
    def _():
        o_ref[...] = x_ref[...]

    o_ref[...] += jnp.dot(u_ref[...], w_ref[...], preferred_element_type=F32)

    if final_norm:
        @pl.when(k == pl.num_programs(1) - 1)
        def _():
            _rms_rows(o_ref, g_ref, o_ref, min(64, o_ref.shape[0]))


def _mlp_down(u, w, x, g, tm, tk, final_norm):
    m, d = x.shape
    f = u.shape[1]
    return pl.pallas_call(
        functools.partial(_mlp_down_kernel, final_norm=final_norm),
        grid=(m // tm, f // tk),
        in_specs=[
            pl.BlockSpec((tm, tk), lambda i, k: (i, k)),
            pl.BlockSpec((tk, d), lambda i, k: (k, 0)),
            pl.BlockSpec((tm, d), lambda i, k: (i, 0), pipeline_mode=pl.Buffered(1)),
            pl.BlockSpec((1, d), lambda i, k: (0, 0)),
        ],
        out_specs=pl.BlockSpec((tm, d), lambda i, k: (i, 0)),
        out_shape=jax.ShapeDtypeStruct((m, d), F32),
        compiler_params=_cparams(("parallel", "arbitrary")),
        name="mlp_down",
    )(u, w, x, g)


def _nt_dot(a, b):
    return lax.dot_general(a, b, (((1,), (1,)), ((), ())), preferred_element_type=F32)


def _sidx_kernel(pt_ref, q_ref, w_ref, knew_ref, *rest):
    n_pages = len(rest) - 2
    page_refs, sc_ref, self_ref = rest[:n_pages], rest[n_pages], rest[n_pages + 1]
    q = q_ref[...]
    w = w_ref[...] * (IDX_DIM ** -0.5)
    for r in range(n_pages):
        lg = _nt_dot(q, page_refs[r][...].astype(BF16))
        sc_ref[:, r * PAGE_SIZE:(r + 1) * PAGE_SIZE] = jnp.sum(jnp.maximum(lg, 0.0) * w, axis=0, keepdims=True)
    kn = knew_ref[...].astype(BF16).astype(F32)
    lg = jnp.sum(q.astype(F32) * kn, axis=1, keepdims=True)
    self_ref[...] = jnp.broadcast_to(jnp.sum(jnp.maximum(lg, 0.0) * w, axis=0, keepdims=True), self_ref.shape)


def _sample_scores(page_table, qi_s, wcol, ki_new, cache_k_idx_l):
    db, n_pages = page_table.shape
    page_specs = [pl.BlockSpec((None, PAGE_SIZE, IDX_DIM), functools.partial(lambda b, pt, r: (pt[b, r], 0, 0), r=r))
                  for r in range(n_pages)]
    return pl.pallas_call(
        _sidx_kernel,
        grid_spec=pltpu.PrefetchScalarGridSpec(
            num_scalar_prefetch=1,
            grid=(db,),
            in_specs=[
                pl.BlockSpec((None, N_IDX_HEADS, IDX_DIM), lambda b, pt: (b, 0, 0)),
                pl.BlockSpec((None, N_IDX_HEADS, 1), lambda b, pt: (b, 0, 0)),
                pl.BlockSpec((None, 1, IDX_DIM), lambda b, pt: (b, 0, 0)),
            ] + page_specs,
            out_specs=[
                pl.BlockSpec((None, 1, n_pages * PAGE_SIZE), lambda b, pt: (b, 0, 0)),
                pl.BlockSpec((None, 1, LANES), lambda b, pt: (b, 0, 0)),
            ],
        ),
        out_shape=[jax.ShapeDtypeStruct((db, 1, n_pages * PAGE_SIZE), F32),
                   jax.ShapeDtypeStruct((db, 1, LANES), F32)],
        compiler_params=_cparams(("parallel",)),
        name="sample_scores",
    )(page_table, qi_s, wcol, ki_new, *([cache_k_idx_l] * n_pages))


def _stopk_kernel(sc_ref, self_ref, bias_ref, bself_ref, *, topk):
    db, past = sc_ref.shape
    keys = _sort_key(sc_ref[...])
    kself = _sort_key(self_ref[...])

    def count_ge(cand):
        c = jnp.sum(jnp.where(keys >= cand, 1.0, 0.0), axis=1, keepdims=True)
        return c + jnp.where(kself[:, 0:1] >= cand, 1.0, 0.0)

    thr = _kth_largest(count_ge, topk, db)
    bias_ref[...] = jnp.where(keys >= thr, 0.0, NEG)
    bself_ref[...] = jnp.where(kself >= thr, 0.0, NEG)


def _sample_topk(scores, self_scores, topk):
    db, past = scores.shape
    return pl.pallas_call(
        functools.partial(_stopk_kernel, topk=topk),
        out_shape=[jax.ShapeDtypeStruct((db, past), F32), jax.ShapeDtypeStruct((db, LANES), F32)],
        compiler_params=pltpu.CompilerParams(vmem_limit_bytes=VMEM_LIMIT),
        name="sample_topk",
    )(scores, self_scores)


def _sattn_kernel(pt_ref, qa_ref, qb_ref, bias_ref, bself_ref, kas_ref, vas_ref, kbs_ref, vbs_ref,
                  lq1, lk1, lq2, lk2, gn_ref, *rest, pg, lam_init):
    ka_refs, va_refs, kb_refs, vb_refs = rest[0:pg], rest[pg:2 * pg], rest[2 * pg:3 * pg], rest[3 * pg:4 * pg]
    oa_ref, ob_ref, ma, la, acca, mb, lb, accb = rest[4 * pg:]
    step = pl.program_id(1)
    nh = N_HEADS_A

    @pl.when(step == 0)
    def _():
        ma[...] = jnp.full(ma.shape, NEG, F32)
        mb[...] = jnp.full(mb.shape, NEG, F32)
        la[...] = jnp.zeros(la.shape, F32)
        lb[...] = jnp.zeros(lb.shape, F32)
        acca[...] = jnp.zeros(acca.shape, F32)
        accb[...] = jnp.zeros(accb.shape, F32)

    qa = qa_ref[...]
    qb = qb_ref[...]
    row = lax.broadcasted_iota(jnp.int32, (nh, LANES), 0)
    grp_a = row // (N_HEADS_A // N_KV_A)
    grp_b = (row % N_HEADS_B) // (N_HEADS_B // N_KV_B)
    cmp_b = row // N_HEADS_B

    sa, sb = [], []
    for r in range(pg):
        s = jnp.zeros((nh, PAGE_SIZE), F32)
        for g in range(N_KV_A):
            kg = ka_refs[r][pl.ds(g, PAGE_SIZE, stride=N_KV_A), :].astype(BF16)
            s = jnp.where(grp_a == g, _nt_dot(qa, kg), s)
        sa.append(s + bias_ref[:, r * PAGE_SIZE:(r + 1) * PAGE_SIZE])
        s = jnp.zeros((nh, PAGE_SIZE), F32)
        for g in range(N_KV_B):
            for c in range(2):
                kg = kb_refs[r][pl.ds(g * 2 + c, PAGE_SIZE, stride=2 * N_KV_B), :].astype(BF16)
                s = jnp.where((grp_b == g) & (cmp_b == c), _nt_dot(qb, kg), s)
        sb.append(s)
    sa = jnp.concatenate(sa, axis=1)
    sb = jnp.concatenate(sb, axis=1)

    pa, alpha_a, m_new, l_new = _online_softmax_step(sa, ma[...], la[...])
    ma[...] = m_new
    la[...] = l_new
    pb, alpha_b, m_new, l_new = _online_softmax_step(sb, mb[...], lb[...])
    mb[...] = m_new
    lb[...] = l_new
    pa = pa.astype(BF16)
    pb = pb.astype(BF16)
    rowv = lax.broadcasted_iota(jnp.int32, (nh, 2 * HEAD_DIM), 0)
    grp_bv = (rowv % N_HEADS_B) // (N_HEADS_B // N_KV_B)
    oa = jnp.zeros((nh, HEAD_DIM), F32)
    ob = jnp.zeros((nh, 2 * HEAD_DIM), F32)
    for r in range(pg):
        par = pa[:, r * PAGE_SIZE:(r + 1) * PAGE_SIZE]
        pbr = pb[:, r * PAGE_SIZE:(r + 1) * PAGE_SIZE]
        ta = jnp.zeros((nh, HEAD_DIM), F32)
        tb = jnp.zeros((nh, 2 * HEAD_DIM), F32)
        for g in range(N_KV_A):
            vg = va_refs[r][pl.ds(g, PAGE_SIZE, stride=N_KV_A), :].astype(BF16)
            ta = jnp.where(grp_a == g, jnp.dot(par, vg, preferred_element_type=F32), ta)
        for g in range(N_KV_B):
            vg = jnp.concatenate([vb_refs[r][pl.ds(cb * N_KV_B + g, PAGE_SIZE, stride=2 * N_KV_B), :]
                                  for cb in range(2)], axis=1).astype(BF16)
            tb = jnp.where(grp_bv == g, jnp.dot(pbr, vg, preferred_element_type=F32), tb)
        oa = oa + ta
        ob = ob + tb
    acca[...] = alpha_a * acca[...] + oa
    accb[...] = alpha_b * accb[...] + ob

    @pl.when(step == pl.num_programs(1) - 1)
    def _():
        rnd = lambda ref: ref[...].astype(BF16).astype(F32)
        s_self = jnp.sum(qa.astype(F32) * rnd(kas_ref), axis=1, keepdims=True) + bself_ref[:, 0:1]
        m_new = jnp.maximum(ma[...], s_self)
        p = jnp.exp2(s_self - m_new)
        alpha = jnp.exp2(ma[...] - m_new)
        l = alpha * la[...] + p
        o = alpha * acca[...] + p.astype(BF16).astype(F32) * rnd(vas_ref)
        oa_ref[...] = (o / l).astype(oa_ref.dtype)

        s_self = jnp.sum(qb.astype(F32) * rnd(kbs_ref), axis=1, keepdims=True)
        m_new = jnp.maximum(mb[...], s_self)
        p = jnp.exp2(s_self - m_new)
        alpha = jnp.exp2(mb[...] - m_new)
        l = alpha * lb[...] + p
        o = (alpha * accb[...] + p.astype(BF16).astype(F32) * rnd(vbs_ref)) / l
        lam = _lambda(lq1, lk1, lq2, lk2, lam_init)
        o = o[0:N_HEADS_B] - lam * o[N_HEADS_B:]
        ms = jnp.mean(o * o, axis=-1, keepdims=True)
        ob_ref[...] = (o * lax.rsqrt(ms + EPS) * gn_ref[...] * (1.0 - lam_init)).astype(ob_ref.dtype)


def _sample_attention(page_table, qa_s, qb_s, bias, bself, kas, vas, kbs, vbs, lams, gn,
                      ka2d, va2d, kb2d, vb2d, pg, lam_init):
    db, n_pages = page_table.shape
    nh = N_HEADS_A
    vw = 2 * HEAD_DIM

    def page_spec(shape, r):
        nd = len(shape)
        return pl.BlockSpec(shape, functools.partial(
            lambda b, s, pt, r: (pt[b, s * pg + r],) + (0,) * (nd - 1), r=r))

    per_b = lambda shape: pl.BlockSpec((None,) + shape, lambda b, s, pt: (b,) + (0,) * len(shape))
    vec = pl.BlockSpec((1, HEAD_DIM), lambda b, s, pt: (0, 0))
    in_specs = [
        per_b((nh, HEAD_DIM)), per_b((nh, HEAD_DIM)),
        pl.BlockSpec((None, 1, pg * PAGE_SIZE), lambda b, s, pt: (b, 0, s)),
        per_b((1, LANES)),
        per_b((nh, HEAD_DIM)), per_b((nh, HEAD_DIM)), per_b((nh, HEAD_DIM)), per_b((nh, vw)),
        vec, vec, vec, vec,
        pl.BlockSpec((1, vw), lambda b, s, pt: (0, 0)),
    ]
    in_specs += [page_spec((PAGE_SIZE * N_KV_A, HEAD_DIM), r) for r in range(pg)]
    in_specs += [page_spec((PAGE_SIZE * N_KV_A, HEAD_DIM), r) for r in range(pg)]
    in_specs += [page_spec((PAGE_SIZE * N_KV_B * 2, HEAD_DIM), r) for r in range(pg)]
    in_specs += [page_spec((PAGE_SIZE * N_KV_B * 2, HEAD_DIM), r) for r in range(pg)]
    return pl.pallas_call(
        functools.partial(_sattn_kernel, pg=pg, lam_init=lam_init),
        grid_spec=pltpu.PrefetchScalarGridSpec(
            num_scalar_prefetch=1,
            grid=(db, n_pages // pg),
            in_specs=in_specs,
            out_specs=[per_b((nh, HEAD_DIM)), per_b((N_HEADS_B, vw))],
            scratch_shapes=[
                pltpu.VMEM((nh, 1), F32), pltpu.VMEM((nh, 1), F32), pltpu.VMEM((nh, HEAD_DIM), F32),
                pltpu.VMEM((nh, 1), F32), pltpu.VMEM((nh, 1), F32), pltpu.VMEM((nh, vw), F32),
            ],
        ),
        out_shape=[jax.ShapeDtypeStruct((db, nh, HEAD_DIM), BF16),
                   jax.ShapeDtypeStruct((db, N_HEADS_B, vw), BF16)],
        compiler_params=_cparams(("parallel", "arbitrary")),
        name="sample_attention",
    )(page_table, qa_s, qb_s, bias, bself, kas, vas, kbs, vbs, *lams, gn,
      *([ka2d] * pg), *([va2d] * pg), *([kb2d] * pg), *([vb2d] * pg))


def _rope_tables(pos):
    half = ROT_DIM // 2
    inv = jnp.power(ROPE_THETA, -jnp.arange(half, dtype=F32) / half)
    ang = pos.astype(F32)[:, None] * inv[None, :]
    cos, sin = jnp.cos(ang), jnp.sin(ang)
    n = pos.shape[0]
    z = lambda w: jnp.zeros((n, w), F32)
    cos_t = jnp.concatenate([cos, cos, jnp.ones((n, HEAD_DIM - ROT_DIM), F32)], axis=1)
    s1_t = jnp.concatenate([-sin, z(HEAD_DIM - half)], axis=1)
    s2_t = jnp.concatenate([z(half), sin, z(HEAD_DIM - ROT_DIM)], axis=1)
    ident = (jnp.ones((n, LANES), F32), z(LANES), z(LANES))
    return tuple(jnp.stack([t, e, t * Q_SCALE]) for t, e in zip((cos_t, s1_t, s2_t), ident))


def _tile(n, pref):
    return pref if n % pref == 0 else n


def _split_kv32(kv32):
    w = PROJ_TN
    return kv32[:, 0:w], kv32[:, w:2 * w], kv32[:, 2 * w:4 * w], kv32[:, 4 * w:6 * w]


def kernel(x_prompt, x_sample, cache_k_a, cache_v_a, cache_k_b, cache_v_b, cache_k_idx, page_table, norm1, w_in, k_idx_norm_w, k_idx_norm_b, lambda_q1, lambda_k1, lambda_q2, lambda_k2, diff_norm_w, w_out, norm2, w_up, w_down, norm_f):
    batch, seq, d_model = x_prompt.shape
    db, dec_seq, _ = x_sample.shape
    assert dec_seq == 1
    depth = w_in.shape[0]
    n_pages = page_table.shape[1]
    past = n_pages * PAGE_SIZE
    n_pool = cache_k_a.shape[1]
    topk_p = min(TOPK_MAX, seq // 4)
    topk_s = min(TOPK_MAX, (past + dec_seq) // 4)

    mp = batch * seq
    tm_p = _tile(mp, 512)
    tq_a = _tile(seq, 128)
    tq_b = _tile(seq, 256)
    tq_i = _tile(seq, 256)
    tk_att = _tile(seq, 512)
    tk_idx = _tile(seq, 256)
    pg = 8 if n_pages % 8 == 0 else n_pages

    tab_p = _rope_tables(jnp.arange(seq))
    tab_s = _rope_tables(jnp.full((db,), past, jnp.int32))
    row = lambda a: a.reshape(1, -1)

    xp = x_prompt.reshape(mp, d_model)
    xs = x_sample.reshape(db, d_model)
    st = [[] for _ in range(10)]
    for l in range(depth):
        lam_init = 0.8 - 0.6 * math.exp(-0.3 * l)
        w_in_b = w_in[l, :, :COL_TAIL].astype(BF16)
        w_tail = jnp.pad(w_in[l, :, COL_TAIL:], ((0, 0), (0, TAIL_WIDTH - (w_in.shape[2] - COL_TAIL)))).astype(BF16)
        w_out_b = w_out[l].astype(BF16)
        w_up_b = w_up[l].astype(BF16)
        w_down_b = w_down[l].astype(BF16)
        lams = (row(lambda_q1[l]), row(lambda_k1[l]), row(lambda_q2[l]), row(lambda_k2[l]))
        gn = row(diff_norm_w[l])
        n1, n2 = row(norm1[l]), row(norm2[l])
        lnw, lnb = row(k_idx_norm_w[l]), row(k_idx_norm_b[l])
        last = l == depth - 1

        zq, kv32 = _project(xp, n1, w_in_b, tab_p, tm_p, seq // tm_p)
        ki, kibf, wi = _project_tail(xp, n1, w_tail, tab_p, lnw, lnb, tm_p, seq // tm_p)
        ka, va, kb, vb = _split_kv32(kv32)
        kit = kibf.reshape(batch, seq, IDX_DIM).transpose(0, 2, 1)
        kat = zq[:, COL_KA:COL_VA].reshape(batch, seq, N_KV_A, HEAD_DIM).transpose(0, 2, 3, 1)
        kbt = zq[:, COL_KB:COL_VB].reshape(batch, seq, N_KV_B * 2, HEAD_DIM).transpose(0, 2, 3, 1)
        bias = _prompt_indexer(zq, wi, kit, batch, seq, topk_p, tq_i, tk_idx)
        oa = _dsa_prompt(zq, kat, bias, batch, seq, tq_a, tk_att)
        ob = _diff_prompt(zq, kbt, lams, gn, batch, seq, tq_b, tk_att, lam_init)
        xp = _out_proj(oa, ob, w_out_b, xp, tm_p, 1024)
        u = _mlp_up(xp, n2, w_up_b, tm_p, 1024)
        xp = _mlp_down(u, w_down_b, xp, row(norm_f), tm_p, 1024, last)
        st[0].append(ka.reshape(batch, seq, N_KV_A, HEAD_DIM))
        st[1].append(va.reshape(batch, seq, N_KV_A, HEAD_DIM))
        st[2].append(kb.reshape(batch, seq, N_KV_B, 2, HEAD_DIM))
        st[3].append(vb.reshape(batch, seq, N_KV_B, 2 * HEAD_DIM))
        st[4].append(ki.reshape(batch, seq, IDX_DIM))

        zq_s, kv32_s = _project(xs, n1, w_in_b, tab_s, db, 1)
        ki_s, _, wi_s = _project_tail(xs, n1, w_tail, tab_s, lnw, lnb, db, 1)
        ka_s, va_s, kb_s, vb_s = _split_kv32(kv32_s)
        qi_s = zq_s[:, COL_QI:COL_TAIL].reshape(db, N_IDX_HEADS, IDX_DIM)
        wcol = wi_s[:, :N_IDX_HEADS].reshape(db, N_IDX_HEADS, 1)
        scores, self_sc = _sample_scores(page_table, qi_s, wcol, ki_s.reshape(db, 1, IDX_DIM), cache_k_idx[l])
        bias_s, bself = _sample_topk(scores.reshape(db, past), self_sc.reshape(db, LANES), topk_s)
        qa_s = zq_s[:, COL_QA:COL_KA].reshape(db, N_HEADS_A, HEAD_DIM)
        qb_s = zq_s[:, COL_QB:COL_KB].reshape(db, N_HEADS_B, 2, HEAD_DIM).transpose(0, 2, 1, 3).reshape(
            db, 2 * N_HEADS_B, HEAD_DIM)
        rows_a = jnp.arange(N_HEADS_A) // (N_HEADS_A // N_KV_A)
        rr = jnp.arange(2 * N_HEADS_B)
        grp_b = (rr % N_HEADS_B) // (N_HEADS_B // N_KV_B)
        rows_kb = grp_b * 2 + rr // N_HEADS_B
        kas = ka_s.reshape(db, N_KV_A, HEAD_DIM)[:, rows_a]
        vas = va_s.reshape(db, N_KV_A, HEAD_DIM)[:, rows_a]
        kbs = kb_s.reshape(db, N_KV_B * 2, HEAD_DIM)[:, rows_kb]
        vbs = vb_s.reshape(db, N_KV_B, 2 * HEAD_DIM)[:, grp_b]
        rows_per_page = PAGE_SIZE * N_KV_B * 2
        vb2d = cache_v_b[l].reshape(n_pool, PAGE_SIZE, N_KV_B, 2, HEAD_DIM).transpose(0, 1, 3, 2, 4).reshape(
            n_pool * rows_per_page, HEAD_DIM)
        oa_s, ob_s = _sample_attention(
            page_table, qa_s, qb_s, bias_s.reshape(db, 1, past), bself.reshape(db, 1, LANES), kas, vas, kbs, vbs,
            lams, gn,
            cache_k_a[l].reshape(n_pool * PAGE_SIZE * N_KV_A, HEAD_DIM),
            cache_v_a[l].reshape(n_pool * PAGE_SIZE * N_KV_A, HEAD_DIM),
            cache_k_b[l].reshape(n_pool * rows_per_page, HEAD_DIM),
            vb2d, pg, lam_init)
        xs = _out_proj(oa_s.reshape(db, MIX_A), ob_s.reshape(db, MIX_B), w_out_b, xs, db, 1024)
        u_s = _mlp_up(xs, n2, w_up_b, db, 1024)
        xs = _mlp_down(u_s, w_down_b, xs, row(norm_f), db, 1024, last)
        st[5].append(ka_s.reshape(db, dec_seq, N_KV_A, HEAD_DIM))
        st[6].append(va_s.reshape(db, dec_seq, N_KV_A, HEAD_DIM))
        st[7].append(kb_s.reshape(db, dec_seq, N_KV_B, 2, HEAD_DIM))
        st[8].append(vb_s.reshape(db, dec_seq, N_KV_B, 2 * HEAD_DIM))
        st[9].append(ki_s.reshape(db, dec_seq, IDX_DIM))
    outs = [jnp.stack(s, axis=0) for s in st]
    return (xp.reshape(batch, seq, d_model), xs.reshape(db, dec_seq, d_model), *outs)
```
